```python
import jax, jax.numpy as jnp
from jax import lax
import numpy as np

D_MODEL = 1024
BATCH = 4
SEQ = 8192
DEPTH = 2

NORM_EPS = 1e-6
GN_EPS = 1e-5
NEG_INF = -1e30

GLA_HEADS = 4
GLA_DK = 128
GLA_DV = 256
GLA_GATE_RANK = 16
GLA_TAU = 16.0
GLA_CHUNK = 64

DIL_HEADS = 8
DIL_HD = 64
DIL_PATTERNS = ((128, 1), (512, 4), (2048, 16))
DIL_BLOCK = 128

RET_HEADS = 4
RET_DK = 256
RET_DV = 512
RET_CHUNK = 128

FFN_DENSE = 2816
N_EXPERTS = 8
TOP_K = 2
FFN_EXPERT = 3584
MOE_BLOCK = 256

EVEN_COLS = (GLA_HEADS * GLA_DK, GLA_HEADS * GLA_DK, GLA_HEADS * GLA_DV, GLA_HEADS * GLA_DV,
             GLA_GATE_RANK, DIL_HEADS * DIL_HD, DIL_HEADS * DIL_HD, DIL_HEADS * DIL_HD)
EVEN_IN = sum(EVEN_COLS)
EVEN_MIX = GLA_HEADS * GLA_DV + DIL_HEADS * DIL_HD
ODD_COLS = (RET_HEADS * RET_DK, RET_HEADS * RET_DK, RET_HEADS * RET_DV, RET_HEADS * RET_DV)
ODD_IN = sum(ODD_COLS)
ODD_MIX = RET_HEADS * RET_DV
N_EVEN = (DEPTH + 1) // 2
N_ODD = DEPTH // 2

kernel_name = 'hybrid_gla_dilated_retention_moe'


def rms_norm(x, w):
    xf = x.astype(jnp.float32)
    return xf * lax.rsqrt(jnp.mean(xf * xf, axis=-1, keepdims=True) + NORM_EPS) * w.astype(jnp.float32)


def split_cols(h, cols):
    return jnp.split(h, [int(c) for c in np.cumsum(cols)[:-1]], axis=-1)


def to_chunks(t, n_heads, d, chunk):
    b, s, _ = t.shape
    return t.astype(jnp.float32).reshape(b, s // chunk, chunk, n_heads, d).transpose(0, 3, 1, 2, 4)


def from_chunks(t):
    b, h, n, c, d = t.shape
    return t.transpose(0, 2, 3, 1, 4).reshape(b, n * c, h, d)


def swiglu(x, w_gate, w_up, w_down):
    return (jax.nn.silu(x @ w_gate) * (x @ w_up)) @ w_down


def gla_mixer(q, k, v, r, gate_lr, w_gate_up, b_gate, out_norm):
    bsz, s, _ = q.shape
    h, dk, dv, c = GLA_HEADS, GLA_DK, GLA_DV, GLA_CHUNK
    log_a = jax.nn.log_sigmoid((gate_lr @ w_gate_up + b_gate).astype(jnp.float32)) / GLA_TAU
    qh = to_chunks(q, h, dk, c) * dk ** -0.5
    kh = to_chunks(k, h, dk, c)
    vh = to_chunks(v, h, dv, c)
    cum = jnp.cumsum(to_chunks(log_a, h, dk, c), axis=3)
    total = cum[:, :, :, -1]
    q_dec = qh * jnp.exp(cum)
    k_inv = kh * jnp.exp(-cum)
    k_end = kh * jnp.exp(total[:, :, :, None] - cum)
    causal = jnp.tril(jnp.ones((c, c), dtype=bool))
    att = jnp.where(causal, jnp.einsum('bhnik,bhnjk->bhnij', q_dec, k_inv), 0.0)
    o_intra = jnp.einsum('bhnij,bhnjv->bhniv', att, vh)

    def step(state, inp):
        q_c, k_c, v_c, g_c = inp
        o_c = jnp.einsum('bhik,bhkv->bhiv', q_c, state)
        state = state * jnp.exp(g_c)[..., None] + jnp.einsum('bhik,bhiv->bhkv', k_c, v_c)
        return state, o_c

    xs = tuple(jnp.moveaxis(t, 2, 0) for t in (q_dec, k_end, vh, total))
    _, o_inter = lax.scan(step, jnp.zeros((bsz, h, dk, dv), jnp.float32), xs)
    o = from_chunks(o_intra + jnp.moveaxis(o_inter, 0, 2))
    gate = jax.nn.silu(r.astype(jnp.float32)).reshape(bsz, s, h, dv)
    return (rms_norm(o, out_norm) * gate).reshape(bsz, s, h * dv)


def dilated_pattern(q, k, v, slopes, window, dilation):
    bsz, h, s, hd = q.shape
    blk = DIL_BLOCK
    span = window // dilation
    seg = s // dilation
    nb = -(-seg // blk)
    seg_p = nb * blk

    def to_residue(t):
        t = t.reshape(bsz, h, seg, dilation, hd).transpose(0, 1, 3, 2, 4)
        return jnp.pad(t, ((0, 0), (0, 0), (0, 0), (0, seg_p - seg), (0, 0)))

    def band(t):
        t = jnp.pad(t, ((0, 0), (0, 0), (0, 0), (blk, 0), (0, 0))).reshape(bsz, h, dilation, nb + 1, blk, hd)
        return jnp.concatenate([t[:, :, :, :-1], t[:, :, :, 1:]], axis=4)

    qb = to_residue(q).reshape(bsz, h, dilation, nb, blk, hd)
    kb = band(to_residue(k))
    vb = band(to_residue(v))
    steps = (jnp.arange(blk)[:, None] + blk) - jnp.arange(2 * blk)[None, :]
    key_pos = (jnp.arange(nb) * blk - blk)[:, None, None] + jnp.arange(2 * blk)[None, None, :]
    valid = (steps >= 0) & (steps <= span) & (key_pos >= 0)
    bias = -slopes[:, None, None, None, None] * (steps * dilation).astype(jnp.float32)
    sc = jnp.einsum('bhrnqd,bhrnkd->bhrnqk', qb, kb) * hd ** -0.5 + bias
    sc = jnp.where(valid, sc, NEG_INF)
    m = jnp.max(sc, axis=-1, keepdims=True)
    p = jnp.exp(sc - m)
    l = jnp.sum(p, axis=-1, keepdims=True)
    o = jnp.einsum('bhrnqk,bhrnkd->bhrnqd', p, vb) / l
    lse = (m + jnp.log(l))[..., 0]

    def from_residue(t):
        rest = t.shape[5:]
        t = t.reshape((bsz, h, dilation, seg_p) + rest)[:, :, :, :seg]
        return jnp.swapaxes(t, 2, 3).reshape((bsz, h, s) + rest)

    return from_residue(o), from_residue(lse)


def dilated_mixer(q, k, v, q_norm, k_norm):
    bsz, s, _ = q.shape
    h, hd = DIL_HEADS, DIL_HD

    def heads(t, w):
        return rms_norm(t.reshape(bsz, s, h, hd), w).transpose(0, 2, 1, 3)

    qh, kh = heads(q, q_norm), heads(k, k_norm)
    vh = v.astype(jnp.float32).reshape(bsz, s, h, hd).transpose(0, 2, 1, 3)
    slopes = jnp.exp2(-8.0 * jnp.arange(1, h + 1, dtype=jnp.float32) / h)
    outs, lses = zip(*[dilated_pattern(qh, kh, vh, slopes, w, d) for w, d in DIL_PATTERNS])
    mix_w = jax.nn.softmax(jnp.stack(lses), axis=0)
    o = jnp.einsum('pbhs,pbhsd->bshd', mix_w, jnp.stack(outs))
    return o.reshape(bsz, s, h * hd)


def retention_mixer(q, k, v, g, group_norm_w):
    bsz, s, _ = q.shape
    h, dk, dv, c = RET_HEADS, RET_DK, RET_DV, RET_CHUNK
    log_gamma = jnp.log1p(-jnp.exp2(-5.0 - jnp.arange(h, dtype=jnp.float32)))
    qh = to_chunks(q, h, dk, c)
    kh = to_chunks(k, h, dk, c) * dk ** -0.5
    vh = to_chunks(v, h, dv, c)
    pos = jnp.arange(c, dtype=jnp.float32)
    rel = pos[:, None] - pos[None, :]
    decay = jnp.where(rel >= 0, jnp.exp(log_gamma[:, None, None] * jnp.maximum(rel, 0.0)), 0.0)
    att = jnp.einsum('bhnik,bhnjk->bhnij', qh, kh) * decay[:, None]
    o_intra = jnp.einsum('bhnij,bhnjv->bhniv', att, vh)
    q_dec = qh * jnp.exp(log_gamma[:, None] * (pos + 1.0))[:, None, :, None]
    k_dec = kh * jnp.exp(log_gamma[:, None] * (c - 1.0 - pos))[:, None, :, None]
    chunk_decay = jnp.exp(log_gamma * c)[:, None, None]

    def step(state, inp):
        q_c, k_c, v_c = inp
        o_c = jnp.einsum('bhik,bhkv->bhiv', q_c, state)
        state = state * chunk_decay + jnp.einsum('bhik,bhiv->bhkv', k_c, v_c)
        return state, o_c

    xs = tuple(jnp.moveaxis(t, 2, 0) for t in (q_dec, k_dec, vh))
    _, o_inter = lax.scan(step, jnp.zeros((bsz, h, dk, dv), jnp.float32), xs)
    o = from_chunks(o_intra + jnp.moveaxis(o_inter, 0, 2))
    mu = jnp.mean(o, axis=-1, keepdims=True)
    var = jnp.mean(jnp.square(o - mu), axis=-1, keepdims=True)
    o = (o - mu) * lax.rsqrt(var + GN_EPS) * group_norm_w.astype(jnp.float32).reshape(h, dv)
    gate = jax.nn.silu(g.astype(jnp.float32)).reshape(bsz, s, h, dv)
    return (o * gate).reshape(bsz, s, h * dv)


def moe_ffn(x, w_router, w_gate, w_up, w_down):
    bsz, s, d = x.shape
    t = bsz * s
    xf = x.reshape(t, d)
    logits = (xf @ w_router).astype(jnp.float32)
    top_logit, top_idx = lax.top_k(logits, TOP_K)
    gates = jax.nn.softmax(top_logit, axis=-1)
    n_assign = t * TOP_K
    e_flat = top_idx.reshape(n_assign)
    tok_flat = (jnp.arange(n_assign) // TOP_K).astype(jnp.int32)
    g_flat = gates.reshape(n_assign)
    order = jnp.argsort(e_flat)
    e_s, tok_s, g_s = e_flat[order], tok_flat[order], g_flat[order]
    counts = jnp.bincount(e_flat, length=N_EXPERTS)
    starts = jnp.cumsum(counts) - counts
    padded = (counts + MOE_BLOCK - 1) // MOE_BLOCK * MOE_BLOCK
    pends = jnp.cumsum(padded)
    pstarts = pends - padded
    dest = pstarts[e_s] + jnp.arange(n_assign) - starts[e_s]
    n_blocks = -(-n_assign // MOE_BLOCK) + N_EXPERTS
    n_rows = n_blocks * MOE_BLOCK
    buf_tok = jnp.zeros((n_rows,), jnp.int32).at[dest].set(tok_s)
    buf_gate = jnp.zeros((n_rows,), jnp.float32).at[dest].set(g_s)
    block_expert = jnp.minimum(jnp.searchsorted(pends, jnp.arange(n_blocks) * MOE_BLOCK, side='right'), N_EXPERTS - 1)
    xb = xf[buf_tok].reshape(n_blocks, MOE_BLOCK, d)

    def expert_block(args):
        xblk, e = args
        return (jax.nn.silu(xblk @ w_gate[e]) * (xblk @ w_up[e])) @ w_down[e]

    yb = lax.map(expert_block, (xb, block_expert))
    y = yb.reshape(n_rows, d).astype(jnp.float32) * buf_gate[:, None]
    out = jnp.zeros((t, d), jnp.float32).at[buf_tok].add(y)
    return out.astype(x.dtype).reshape(bsz, s, d)


def setup_inputs(seed: int = 0) -> dict:
    key = jax.random.key(seed)
    keys = iter(jax.random.split(key, 24))
    f32 = jnp.float32
    D = D_MODEL

    def dense(shape, fan_in):
        return jax.random.normal(next(keys), shape, f32) * fan_in ** -0.5

    def gain(shape):
        return 1.0 + 0.02 * jax.random.normal(next(keys), shape, f32)

    def small(shape, scale):
        return scale * jax.random.normal(next(keys), shape, f32)

    return {
        'x': jax.random.normal(next(keys), (BATCH, SEQ, D), f32),
        'mix_norm_even': gain((N_EVEN, D)),
        'w_in_even': dense((N_EVEN, D, EVEN_IN), D),
        'w_gla_gate_up': dense((N_EVEN, GLA_GATE_RANK, GLA_HEADS * GLA_DK), GLA_GATE_RANK),
        'b_gla_gate': small((N_EVEN, GLA_HEADS * GLA_DK), 0.1),
        'gla_out_norm': gain((N_EVEN, GLA_DV)),
        'q_norm': gain((N_EVEN, DIL_HD)),
        'k_norm': gain((N_EVEN, DIL_HD)),
        'w_out_even': dense((N_EVEN, EVEN_MIX, D), EVEN_MIX),
        'ffn_norm_even': gain((N_EVEN, D)),
        'w_ffn_gate': dense((N_EVEN, D, FFN_DENSE), D),
        'w_ffn_up': dense((N_EVEN, D, FFN_DENSE), D),
        'w_ffn_down': dense((N_EVEN, FFN_DENSE, D), FFN_DENSE),
        'mix_norm_odd': gain((N_ODD, D)),
        'w_in_odd': dense((N_ODD, D, ODD_IN), D),
        'ret_group_norm': gain((N_ODD, ODD_MIX)),
        'w_out_odd': dense((N_ODD, ODD_MIX, D), ODD_MIX),
        'ffn_norm_odd': gain((N_ODD, D)),
        'w_router': dense((N_ODD, D, N_EXPERTS), D),
        'w_exp_gate': dense((N_ODD, N_EXPERTS, D, FFN_EXPERT), D),
        'w_exp_up': dense((N_ODD, N_EXPERTS, D, FFN_EXPERT), D),
        'w_exp_down': dense((N_ODD, N_EXPERTS, FFN_EXPERT, D), FFN_EXPERT),
    }


def reference(x, mix_norm_even, w_in_even, w_gla_gate_up, b_gla_gate, gla_out_norm, q_norm, k_norm,
              w_out_even, ffn_norm_even, w_ffn_gate, w_ffn_up, w_ffn_down, mix_norm_odd, w_in_odd,
              ret_group_norm, w_out_odd, ffn_norm_odd, w_router, w_exp_gate, w_exp_up, w_exp_down):
    dt = x.dtype
    for layer in range(DEPTH):
        i = layer // 2
        if layer % 2 == 0:
            hcat = rms_norm(x, mix_norm_even[i]).astype(dt) @ w_in_even[i]
            q_a, k_a, v_a, r_a, lr_a, q_b, k_b, v_b = split_cols(hcat, EVEN_COLS)
            o_a = gla_mixer(q_a, k_a, v_a, r_a, lr_a, w_gla_gate_up[i], b_gla_gate[i], gla_out_norm[i])
            o_b = dilated_mixer(q_b, k_b, v_b, q_norm[i], k_norm[i])
            x = x + jnp.concatenate([o_a, o_b], axis=-1).astype(dt) @ w_out_even[i]
            x = x + swiglu(rms_norm(x, ffn_norm_even[i]).astype(dt), w_ffn_gate[i], w_ffn_up[i], w_ffn_down[i])
        else:
            hcat = rms_norm(x, mix_norm_odd[i]).astype(dt) @ w_in_odd[i]
            q_c, k_c, v_c, g_c = split_cols(hcat, ODD_COLS)
            o_c = retention_mixer(q_c, k_c, v_c, g_c, ret_group_norm[i])
            x = x + o_c.astype(dt) @ w_out_odd[i]
            x = x + moe_ffn(rms_norm(x, ffn_norm_odd[i]).astype(dt), w_router[i], w_exp_gate[i], w_exp_up[i], w_exp_down[i])
    return x
```

```python
import functools

import numpy as np
import jax
import jax.numpy as jnp
from jax import lax
from jax.experimental import pallas as pl
from jax.experimental.pallas import tpu as pltpu

BF16 = jnp.bfloat16
F32 = jnp.float32

D_MODEL = 1024
NORM_EPS = 1e-6
GN_EPS = 1e-5
NEG_INF = -1e30

GLA_HEADS, GLA_DK, GLA_DV, GLA_RANK, GLA_TAU, GLA_CHUNK = 4, 128, 256, 16, 16.0, 64
DIL_HEADS, DIL_HD, DIL_BLOCK = 8, 64, 128
DIL_PATTERNS = ((128, 1), (512, 4), (2048, 16))
RET_HEADS, RET_DK, RET_DV, RET_CHUNK = 4, 256, 512, 128
FFN_DENSE = 2816
N_EXPERTS, FFN_EXPERT = 8, 3584

LANES = 128
SUBLANES = 8
ROW_CHUNKS = D_MODEL // LANES
VMEM_LIMIT = 56 * 1024 * 1024

ROW_TILE = 512
GLA_STEP = 256
FFN_CHUNK = 256
MOE_ROWS = 512
MOE_FCHUNK = 512
GATHER_ROWS = 256

QK_DIL = DIL_HEADS * DIL_HD
GLA_QK = GLA_HEADS * GLA_DK
GLA_V = GLA_HEADS * GLA_DV
EVEN_W = 2 * GLA_QK + 2 * GLA_V + 3 * QK_DIL + LANES
ODD_W = 2 * RET_HEADS * RET_DK + 2 * RET_HEADS * RET_DV


def _dot(a, b):
    return jnp.dot(a, b, preferred_element_type=F32)


def _dot_nt(a, b):
    return lax.dot_general(a, b, (((1,), (1,)), ((), ())), preferred_element_type=F32)


def _dot_tn(a, b):
    return lax.dot_general(a, b, (((0,), (0,)), ((), ())), preferred_element_type=F32)


def _split(x):
    hi = x.astype(BF16)
    lo = (x - hi.astype(F32)).astype(BF16)
    return hi, lo


def _rms(x, gain, eps=NORM_EPS):
    ms = jnp.mean(x * x, axis=-1, keepdims=True)
    return x * lax.rsqrt(ms + eps) * gain


def _silu(x):
    return x / (1.0 + jnp.exp(-x))


def _log_sigmoid(z):
    return jnp.minimum(z, 0.0) - jnp.log1p(jnp.exp(-jnp.abs(z)))


def _params(sem):
    return pltpu.CompilerParams(dimension_semantics=sem, vmem_limit_bytes=VMEM_LIMIT)


def _resident(shape):
    nd = len(shape)
    return pl.BlockSpec(shape, lambda *_: (0,) * nd, pipeline_mode=pl.Buffered(1))


def _head_inv_rms(acc, grp_ref, grpt_ref):
    hi, lo = _split(acc * acc)
    ss = _dot(hi, grp_ref[...]) + _dot(lo, grp_ref[...])
    inv = lax.rsqrt(ss * (1.0 / DIL_HD) + NORM_EPS)
    ih, il = _split(inv)
    return _dot(ih, grpt_ref[...]) + _dot(il, grpt_ref[...])


def _in_even_kernel(x_ref, g_ref, w_ref, qn_ref, kn_ref, grp_ref, grpt_ref,
                    qk_ref, v_ref, r_ref, qb_ref, kb_ref, vb_ref, lr_ref):
    xn = _rms(x_ref[...], g_ref[...]).astype(BF16)
    c = 0
    for ref, width in ((qk_ref, 2 * GLA_QK), (v_ref, GLA_V), (r_ref, GLA_V)):
        ref[...] = _dot(xn, w_ref[:, c:c + width]).astype(BF16)
        c += width
    qb = _dot(xn, w_ref[:, c:c + QK_DIL])
    qb_ref[...] = (qb * _head_inv_rms(qb, grp_ref, grpt_ref) * qn_ref[...]).astype(BF16)
    c += QK_DIL
    kb = _dot(xn, w_ref[:, c:c + QK_DIL])
    kb_ref[...] = (kb * _head_inv_rms(kb, grp_ref, grpt_ref) * kn_ref[...]).astype(BF16)
    c += QK_DIL
    vb_ref[...] = _dot(xn, w_ref[:, c:c + QK_DIL]).astype(BF16)
    c += QK_DIL
    lr_ref[...] = _dot(xn, w_ref[:, c:c + LANES])


def _in_even(x2d, gain, w, qn, kn, grp, grpt):
    t = x2d.shape[0]
    tm = ROW_TILE
    row = lambda width: pl.BlockSpec((tm, width), lambda i: (i, 0))
    out_shape = (
        jax.ShapeDtypeStruct((t, 2 * GLA_QK), BF16),
        jax.ShapeDtypeStruct((t, GLA_V), BF16),
        jax.ShapeDtypeStruct((t, GLA_V), BF16),
        jax.ShapeDtypeStruct((t, QK_DIL), BF16),
        jax.ShapeDtypeStruct((t, QK_DIL), BF16),
        jax.ShapeDtypeStruct((t, QK_DIL), BF16),
        jax.ShapeDtypeStruct((t, LANES), F32),
    )
    return pl.pallas_call(
        _in_even_kernel,
        grid=(t // tm,),
        in_specs=[row(D_MODEL), _resident((1, D_MODEL)), _resident((D_MODEL, EVEN_W)),
                  _resident((1, QK_DIL)), _resident((1, QK_DIL)),
                  _resident((QK_DIL, LANES)), _resident((LANES, QK_DIL))],
        out_specs=[row(2 * GLA_QK), row(GLA_V), row(GLA_V), row(QK_DIL), row(QK_DIL),
                   row(QK_DIL), row(LANES)],
        out_shape=out_shape,
        compiler_params=_params(("parallel",)),
        name="in_even",
    )(x2d, gain, w, qn, kn, grp, grpt)


def _gla_kernel(qk_ref, v_ref, r_ref, lr_ref, wgu_ref, bg_ref, on_ref, o_ref, st_ref):
    c = GLA_CHUNK

    @pl.when(pl.program_id(1) == 0)
    def _():
        st_ref[...] = jnp.zeros_like(st_ref)

    rows = lax.broadcasted_iota(jnp.int32, (c, c), 0)
    cols = lax.broadcasted_iota(jnp.int32, (c, c), 1)
    causal = rows >= cols
    tri = causal.astype(BF16)

    def chunk(ci, carry):
        r0 = pl.multiple_of(ci * c, c)
        rs = pl.ds(r0, c)
        z = _dot(lr_ref[rs, :].astype(BF16), wgu_ref[...]) + bg_ref[...]
        log_a = _log_sigmoid(z) * (1.0 / GLA_TAU)
        hi, lo = _split(log_a)
        cum = _dot(tri, hi) + _dot(tri, lo)
        total = cum[c - 1:c, :]
        q = qk_ref[rs, 0:GLA_QK].astype(F32) * GLA_DK ** -0.5
        k = qk_ref[rs, GLA_QK:2 * GLA_QK].astype(F32)
        q_dec = (q * jnp.exp(cum)).astype(BF16)
        k_inv = (k * jnp.exp(-cum)).astype(BF16)
        k_end = (k * jnp.exp(total - cum)).astype(BF16)
        e_tot = jnp.exp(total)
        for h in range(GLA_HEADS):
            ks = slice(h * GLA_DK, (h + 1) * GLA_DK)
            vs = slice(h * GLA_DV, (h + 1) * GLA_DV)
            v_h = v_ref[rs, vs]
            att = jnp.where(causal, _dot_nt(q_dec[:, ks], k_inv[:, ks]), 0.0).astype(BF16)
            st = st_ref[h]
            o = _dot(att, v_h) + _dot_nt(q_dec[:, ks], st.astype(BF16))
            st_ref[h] = st * e_tot[:, ks] + _dot_tn(v_h, k_end[:, ks])
            gate = _silu(r_ref[rs, vs].astype(F32))
            o_ref[rs, vs] = (_rms(o, on_ref[...]) * gate).astype(BF16)
        return carry

    lax.fori_loop(0, GLA_STEP // c, chunk, 0)


def _gla(qk, v, r, lr, wgu, bg, on, bsz, seq):
    t = qk.shape[0]
    ns = seq // GLA_STEP
    row = lambda width: pl.BlockSpec((GLA_STEP, width), lambda b, s: (b * ns + s, 0))
    return pl.pallas_call(
        _gla_kernel,
        grid=(bsz, ns),
        in_specs=[row(2 * GLA_QK), row(GLA_V), row(GLA_V), row(LANES),
                  _resident((LANES, GLA_QK)), _resident((1, GLA_QK)), _resident((1, GLA_DV))],
        out_specs=row(GLA_V),
        out_shape=jax.ShapeDtypeStruct((t, GLA_V), BF16),
        scratch_shapes=[pltpu.VMEM((GLA_HEADS, GLA_DV, GLA_DK), F32)],
        compiler_params=_params(("parallel", "arbitrary")),
        name="gla",
    )(qk, v, r, lr, wgu, bg, on)


def _dil_kernel(q_ref, kc_ref, vc_ref, kp_ref, vp_ref, o_ref, lse_ref, *, dilation):
    blk = DIL_BLOCK
    has_prev = pl.program_id(2) > 0
    rows = lax.broadcasted_iota(jnp.int32, (blk, blk), 0)
    cols = lax.broadcasted_iota(jnp.int32, (blk, blk), 1)
    dist_cur = ((rows - cols) * dilation).astype(F32)
    dist_prev = dist_cur + float(blk * dilation)
    ok_cur = cols <= rows
    ok_prev = (cols >= rows) & has_prev
    for h in range(DIL_HEADS):
        slope = 2.0 ** (-8.0 * (h + 1) / DIL_HEADS)
        hs = slice(h * DIL_HD, (h + 1) * DIL_HD)
        q = q_ref[0, :, hs]
        s_cur = jnp.where(ok_cur, _dot_nt(q, kc_ref[0, :, hs]) - slope * dist_cur, NEG_INF)
        s_prev = jnp.where(ok_prev, _dot_nt(q, kp_ref[0, :, hs]) - slope * dist_prev, NEG_INF)
        m = jnp.maximum(jnp.max(s_cur, axis=-1, keepdims=True),
                        jnp.max(s_prev, axis=-1, keepdims=True))
        p_cur = jnp.exp(s_cur - m)
        p_prev = jnp.exp(s_prev - m)
        l = jnp.sum(p_cur, axis=-1, keepdims=True) + jnp.sum(p_prev, axis=-1, keepdims=True)
        o = (_dot(p_cur.astype(BF16), vc_ref[0, :, hs])
             + _dot(p_prev.astype(BF16), vp_ref[0, :, hs])) / l
        o_ref[0, :, hs] = o.astype(BF16)
        lse_ref[0, :, hs] = jnp.broadcast_to(m + jnp.log(l), (blk, DIL_HD))


def _dilated_pass(qb, kb, vb, bsz, seq, dilation):
    seg = seq // dilation
    nb = seg // DIL_BLOCK
    view = lambda a: a.reshape(bsz, seg, dilation * QK_DIL)
    cur = pl.BlockSpec((1, DIL_BLOCK, QK_DIL), lambda b, r, n: (b, n, r))
    prev = pl.BlockSpec((1, DIL_BLOCK, QK_DIL), lambda b, r, n: (b, jnp.maximum(n - 1, 0), r))
    o, lse = pl.pallas_call(
        functools.partial(_dil_kernel, dilation=dilation),
        grid=(bsz, dilation, nb),
        in_specs=[cur, cur, cur, prev, prev],
        out_specs=[cur, cur],
        out_shape=(jax.ShapeDtypeStruct((bsz, seg, dilation * QK_DIL), BF16),
                   jax.ShapeDtypeStruct((bsz, seg, dilation * QK_DIL), F32)),
        compiler_params=_params(("parallel", "parallel", "arbitrary")),
        name=f"dilated_d{dilation}",
    )(view(qb), view(kb), view(vb), view(kb), view(vb))
    return o.reshape(bsz * seq, QK_DIL), lse.reshape(bsz * seq, QK_DIL)


def _out_even_kernel(x_ref, oa_ref, o1_ref, o2_ref, o3_ref, l1_ref, l2_ref, l3_ref,
                     wa_ref, wb_ref, y_ref):
    l1, l2, l3 = l1_ref[...], l2_ref[...], l3_ref[...]
    m = jnp.maximum(jnp.maximum(l1, l2), l3)
    e1, e2, e3 = jnp.exp(l1 - m), jnp.exp(l2 - m), jnp.exp(l3 - m)
    ob = (e1 * o1_ref[...].astype(F32) + e2 * o2_ref[...].astype(F32)
          + e3 * o3_ref[...].astype(F32)) / (e1 + e2 + e3)
    y_ref[...] = x_ref[...] + _dot(oa_ref[...], wa_ref[...]) + _dot(ob.astype(BF16), wb_ref[...])


def _out_even(x2d, oa, os_, ls, wa, wb):
    t = x2d.shape[0]
    tm = ROW_TILE
    row = lambda width: pl.BlockSpec((tm, width), lambda i: (i, 0))
    return pl.pallas_call(
        _out_even_kernel,
        grid=(t // tm,),
        in_specs=[row(D_MODEL), row(GLA_V)] + [row(QK_DIL)] * 6
                 + [_resident((GLA_V, D_MODEL)), _resident((QK_DIL, D_MODEL))],
        out_specs=row(D_MODEL),
        out_shape=jax.ShapeDtypeStruct((t, D_MODEL), F32),
        compiler_params=_params(("parallel",)),
        name="out_even",
    )(x2d, oa, *os_, *ls, wa, wb)


def _ffn_kernel(x_ref, g_ref, wg_ref, wu_ref, wd_ref, y_ref, xn_ref):
    x = x_ref[...]
    xn_ref[...] = _rms(x, g_ref[...]).astype(BF16)
    y_ref[...] = x

    def chunk(ci, carry):
        xn = xn_ref[...]
        h = _silu(_dot(xn, wg_ref[ci])) * _dot(xn, wu_ref[ci])
        y_ref[...] += _dot(h.astype(BF16), wd_ref[ci])
        return carry

    lax.fori_loop(0, FFN_DENSE // FFN_CHUNK, chunk, 0)


def _ffn(x2d, gain, wg, wu, wd):
    t = x2d.shape[0]
    tm = ROW_TILE
    nc = FFN_DENSE // FFN_CHUNK
    row = pl.BlockSpec((tm, D_MODEL), lambda i: (i, 0))
    return pl.pallas_call(
        _ffn_kernel,
        grid=(t // tm,),
        in_specs=[row, _resident((1, D_MODEL)), _resident((nc, D_MODEL, FFN_CHUNK)),
                  _resident((nc, D_MODEL, FFN_CHUNK)), _resident((nc, FFN_CHUNK, D_MODEL))],
        out_specs=row,
        out_shape=jax.ShapeDtypeStruct((t, D_MODEL), F32),
        scratch_shapes=[pltpu.VMEM((tm, D_MODEL), BF16)],
        compiler_params=_params(("parallel",)),
        name="ffn_dense",
    )(x2d, gain, wg, wu, wd)


def _in_odd_kernel(x_ref, g_ref, w_ref, h_ref):
    xn = _rms(x_ref[...], g_ref[...]).astype(BF16)
    width = RET_HEADS * RET_DK
    for ci in range(ODD_W // width):
        cs = slice(ci * width, (ci + 1) * width)
        acc = _dot(xn, w_ref[:, cs])
        if ci == 1:
            acc = acc * RET_DK ** -0.5
        h_ref[:, cs] = acc.astype(BF16)


def _in_odd(x2d, gain, w):
    t = x2d.shape[0]
    tm = ROW_TILE
    return pl.pallas_call(
        _in_odd_kernel,
        grid=(t // tm,),
        in_specs=[pl.BlockSpec((tm, D_MODEL), lambda i: (i, 0)), _resident((1, D_MODEL)),
                  _resident((D_MODEL, ODD_W))],
        out_specs=pl.BlockSpec((tm, ODD_W), lambda i: (i, 0)),
        out_shape=jax.ShapeDtypeStruct((t, ODD_W), BF16),
        compiler_params=_params(("parallel",)),
        name="in_odd",
    )(x2d, gain, w)


def _ret_kernel(q_ref, k_ref, v_ref, g_ref, dec_ref, qd_ref, kd_ref, cd_ref, gn_ref,
                o_ref, st_ref):
    @pl.when(pl.program_id(2) == 0)
    def _():
        st_ref[...] = jnp.zeros_like(st_ref)

    q = q_ref[...]
    k = k_ref[...]
    v = v_ref[...]
    att = (_dot_nt(q, k) * dec_ref[0]).astype(BF16)
    q_dec = (q.astype(F32) * qd_ref[0]).astype(BF16)
    k_dec = (k.astype(F32) * kd_ref[0]).astype(BF16)
    st = st_ref[...]
    o = _dot(att, v) + _dot(q_dec, st.astype(BF16))
    st_ref[...] = st * cd_ref[0] + _dot_tn(k_dec, v)
    mu = jnp.mean(o, axis=-1, keepdims=True)
    oc = o - mu
    var = jnp.mean(oc * oc, axis=-1, keepdims=True)
    gate = _silu(g_ref[...].astype(F32))
    o_ref[...] = (oc * lax.rsqrt(var + GN_EPS) * gn_ref[...] * gate).astype(BF16)


def _retention_tables():
    h, c = RET_HEADS, RET_CHUNK
    log_gamma = np.log1p(-np.exp2(-5.0 - np.arange(h, dtype=np.float64)))
    pos = np.arange(c, dtype=np.float64)
    rel = pos[:, None] - pos[None, :]
    decay = np.where(rel >= 0, np.exp(log_gamma[:, None, None] * np.maximum(rel, 0.0)), 0.0)
    q_dec = np.exp(log_gamma[:, None] * (pos + 1.0))[:, :, None]
    k_dec = np.exp(log_gamma[:, None] * (c - 1.0 - pos))[:, :, None]
    chunk_decay = np.broadcast_to(np.exp(log_gamma * c)[:, None, None], (h, 1, RET_DV))
    f = lambda a: jnp.asarray(np.ascontiguousarray(a), F32)
    return f(decay), f(q_dec), f(k_dec), f(chunk_decay)


def _retention(hcat, gn_w, bsz, seq):
    t = hcat.shape[0]
    c = RET_CHUNK
    ns = seq // c
    decay, q_dec, k_dec, chunk_decay = _retention_tables()
    nq = RET_HEADS * RET_DK // RET_DK
    nv = 2 * RET_HEADS * RET_DK // RET_DV
    col = lambda width, base: pl.BlockSpec((c, width), lambda b, h, s: (b * ns + s, base + h))
    per_head = lambda shape: pl.BlockSpec((1,) + shape, lambda b, h, s: (h, 0, 0))
    return pl.pallas_call(
        _ret_kernel,
        grid=(bsz, RET_HEADS, ns),
        in_specs=[col(RET_DK, 0), col(RET_DK, nq), col(RET_DV, nv), col(RET_DV, nv + RET_HEADS),
                  per_head((c, c)), per_head((c, 1)), per_head((c, 1)), per_head((1, RET_DV)),
                  pl.BlockSpec((1, RET_DV), lambda b, h, s: (0, h))],
        out_specs=col(RET_DV, 0),
        out_shape=jax.ShapeDtypeStruct((t, RET_HEADS * RET_DV), BF16),
        scratch_shapes=[pltpu.VMEM((RET_DK, RET_DV), F32)],
        compiler_params=_params(("parallel", "parallel", "arbitrary")),
        name="retention",
    )(hcat, hcat, hcat, hcat, decay, q_dec, k_dec, chunk_decay, gn_w)


def _out_odd_kernel(x_ref, oc_ref, w_ref, g_ref, wrh_ref, wrl_ref, tri_ref,
                    y_ref, xn_ref, meta_ref, gate_ref, cnt_ref, run_ref):
    @pl.when(pl.program_id(0) == 0)
    def _():
        run_ref[...] = jnp.zeros_like(run_ref)

    y = x_ref[...] + _dot(oc_ref[...], w_ref[...])
    y_ref[...] = y
    xn = _rms(y, g_ref[...])
    for s in range(ROW_CHUNKS):
        xn_ref[:, s, :] = xn[:, s * LANES:(s + 1) * LANES]

    hi, lo = _split(xn)
    logits = _dot(hi, wrh_ref[...]) + _dot(lo, wrh_ref[...]) + _dot(hi, wrl_ref[...])
    lane = lax.broadcasted_iota(jnp.int32, logits.shape, 1).astype(F32)
    logits = jnp.where(lane < N_EXPERTS, logits, -jnp.inf)
    t1 = jnp.max(logits, axis=-1, keepdims=True)
    i1 = jnp.min(jnp.where(logits == t1, lane, float(LANES)), axis=-1, keepdims=True)
    rest = jnp.where(lane == i1, -jnp.inf, logits)
    t2 = jnp.max(rest, axis=-1, keepdims=True)
    i2 = jnp.min(jnp.where(rest == t2, lane, float(LANES)), axis=-1, keepdims=True)
    e2 = jnp.exp(t2 - t1)
    den = 1.0 + e2
    gate_ref[...] = jnp.where(lane == 0.0, 1.0 / den, jnp.where(lane == 1.0, e2 / den, 0.0))

    sel = (lane == i1) | (lane == i2)
    rank = _dot(tri_ref[...], sel.astype(BF16)) + run_ref[...]
    r1 = jnp.sum(jnp.where(lane == i1, rank, 0.0), axis=-1, keepdims=True)
    r2 = jnp.sum(jnp.where(lane == i2, rank, 0.0), axis=-1, keepdims=True)
    run_ref[...] += jnp.sum(sel.astype(F32), axis=0, keepdims=True)
    cnt_ref[...] = run_ref[...]
    meta = jnp.where(lane == 0.0, i1, jnp.where(lane == 1.0, i2,
                     jnp.where(lane == 2.0, r1, jnp.where(lane == 3.0, r2, 0.0))))
    meta_ref[...] = meta.astype(jnp.int32)


def _out_odd(x2d, oc, w, gain, wr_hi, wr_lo):
    t = x2d.shape[0]
    tm = ROW_TILE
    tri = jnp.asarray(np.tril(np.ones((tm, tm), np.float32), -1), BF16)
    row = lambda width: pl.BlockSpec((tm, width), lambda i: (i, 0))
    return pl.pallas_call(
        _out_odd_kernel,
        grid=(t // tm,),
        in_specs=[row(D_MODEL), row(RET_HEADS * RET_DV), _resident((RET_HEADS * RET_DV, D_MODEL)),
                  _resident((1, D_MODEL)), _resident((D_MODEL, LANES)),
                  _resident((D_MODEL, LANES)), _resident((tm, tm))],
        out_specs=[row(D_MODEL), pl.BlockSpec((tm, ROW_CHUNKS, LANES), lambda i: (i, 0, 0)),
                   row(LANES), row(LANES), pl.BlockSpec((1, LANES), lambda i: (0, 0))],
        out_shape=(jax.ShapeDtypeStruct((t, D_MODEL), F32),
                   jax.ShapeDtypeStruct((t, ROW_CHUNKS, LANES), F32),
                   jax.ShapeDtypeStruct((t, LANES), jnp.int32),
                   jax.ShapeDtypeStruct((t, LANES), F32),
                   jax.ShapeDtypeStruct((1, LANES), F32)),
        scratch_shapes=[pltpu.VMEM((1, LANES), F32)],
        compiler_params=_params(("arbitrary",)),
        name="out_odd_router",
    )(x2d, oc, w, gain, wr_hi, wr_lo, tri)


def _row_copy(src_ref, src_row, dst_ref, dst_row, sem):
    return pltpu.make_async_copy(src_ref.at[src_row], dst_ref.at[dst_row], sem)


def _dispatch_kernel(dest_ref, x_ref, init_ref, xs_ref, sem):
    del init_ref
    n = GATHER_ROWS
    base = pl.program_id(0) * n

    def issue(j, carry):
        _row_copy(x_ref, base + j, xs_ref, dest_ref[0, 0, j], sem).start()
        _row_copy(x_ref, base + j, xs_ref, dest_ref[0, 0, n + j], sem).start()
        return carry

    lax.fori_loop(0, n, issue, 0, unroll=8)

    def drain(j, carry):
        _row_copy(x_ref, 0, xs_ref, 0, sem).wait()
        return carry

    lax.fori_loop(0, 2 * n, drain, 0, unroll=8)


def _dispatch(xn3, dest_blocks, n_rows):
    t = xn3.shape[0]
    n = GATHER_ROWS
    zeros = jnp.zeros((n_rows, ROW_CHUNKS, LANES), F32)
    return pl.pallas_call(
        _dispatch_kernel,
        grid=(t // n,),
        in_specs=[pl.BlockSpec((1, 1, 2 * n), lambda i: (i, 0, 0), memory_space=pltpu.SMEM),
                  pl.BlockSpec(memory_space=pl.ANY), pl.BlockSpec(memory_space=pl.ANY)],
        out_specs=pl.BlockSpec(memory_space=pl.ANY),
        out_shape=jax.ShapeDtypeStruct((n_rows, ROW_CHUNKS, LANES), F32),
        scratch_shapes=[pltpu.SemaphoreType.DMA(())],
        input_output_aliases={2: 0},
        compiler_params=_params(("arbitrary",)),
        name="moe_dispatch",
    )(dest_blocks, xn3, zeros)


def _moe_kernel(be_ref, nv_ref, x_ref, wg_ref, wu_ref, wd_ref, y_ref, xb_ref, acc_ref):
    del be_ref
    b = pl.program_id(0)
    f = pl.program_id(1)
    nf = pl.num_programs(1)

    @pl.when(f == 0)
    def _():
        acc_ref[...] = jnp.zeros_like(acc_ref)
        for s in range(ROW_CHUNKS):
            xb_ref[:, s * LANES:(s + 1) * LANES] = x_ref[:, s, :].astype(BF16)

    @pl.when(b < nv_ref[0])
    def _():
        x = xb_ref[...]
        h = _silu(_dot(x, wg_ref[0])) * _dot(x, wu_ref[0])
        acc_ref[...] += _dot(h.astype(BF16), wd_ref[0])

    @pl.when(f == nf - 1)
    def _():
        for s in range(ROW_CHUNKS):
            y_ref[:, s, :] = acc_ref[:, s * LANES:(s + 1) * LANES]


def _moe_experts(xs, block_expert, n_valid, wg, wu, wd):
    n_rows = xs.shape[0]
    bm, tf = MOE_ROWS, MOE_FCHUNK
    nf = FFN_EXPERT // tf
    fidx = lambda b, f, nv: jnp.where(b < nv[0], f, nf - 1)
    rows = pl.BlockSpec((bm, ROW_CHUNKS, LANES), lambda b, f, be, nv: (b, 0, 0))
    grid_spec = pltpu.PrefetchScalarGridSpec(
        num_scalar_prefetch=2,
        grid=(n_rows // bm, nf),
        in_specs=[rows,
                  pl.BlockSpec((1, D_MODEL, tf), lambda b, f, be, nv: (be[b], 0, fidx(b, f, nv))),
                  pl.BlockSpec((1, D_MODEL, tf), lambda b, f, be, nv: (be[b], 0, fidx(b, f, nv))),
                  pl.BlockSpec((1, tf, D_MODEL), lambda b, f, be, nv: (be[b], fidx(b, f, nv), 0))],
        out_specs=rows,
        scratch_shapes=[pltpu.VMEM((bm, D_MODEL), BF16), pltpu.VMEM((bm, D_MODEL), F32)],
    )
    return pl.pallas_call(
        _moe_kernel,
        grid_spec=grid_spec,
        out_shape=jax.ShapeDtypeStruct((n_rows, ROW_CHUNKS, LANES), F32),
        compiler_params=_params(("arbitrary", "arbitrary")),
        name="moe_experts",
    )(block_expert, n_valid, xs, wg, wu, wd)


def _combine_kernel(dcur_ref, dnext_ref, x_ref, gate_ref, y_ref, o_ref, buf_ref, sem):
    n = GATHER_ROWS
    i = pl.program_id(0)
    slot = i % 2

    def issue(dref, sl):
        def body(j, carry):
            _row_copy(y_ref, dref[0, 0, j], buf_ref.at[sl], j, sem.at[sl]).start()
            return carry
        lax.fori_loop(0, 2 * n, body, 0, unroll=8)

    @pl.when(i == 0)
    def _():
        issue(dcur_ref, 0)

    @pl.when(i + 1 < pl.num_programs(0))
    def _():
        issue(dnext_ref, 1 - slot)

    def drain(j, carry):
        _row_copy(y_ref, 0, buf_ref.at[slot], 0, sem.at[slot]).wait()
        return carry

    lax.fori_loop(0, 2 * n, drain, 0, unroll=8)

    g1 = gate_ref[:, 0:1]
    g2 = gate_ref[:, 1:2]
    for s in range(ROW_CHUNKS):
        cs = slice(s * LANES, (s + 1) * LANES)
        o_ref[:, cs] = (x_ref[:, cs] + g1 * buf_ref[slot, 0:n, s, :]
                        + g2 * buf_ref[slot, n:2 * n, s, :])


def _combine(x3, gates, ys, dest_blocks):
    t = x3.shape[0]
    n = GATHER_ROWS
    nt = t // n
    idx = lambda f: pl.BlockSpec((1, 1, 2 * n), f, memory_space=pltpu.SMEM)
    return pl.pallas_call(
        _combine_kernel,
        grid=(nt,),
        in_specs=[idx(lambda i: (i, 0, 0)), idx(lambda i: (jnp.minimum(i + 1, nt - 1), 0, 0)),
                  pl.BlockSpec((n, D_MODEL), lambda i: (i, 0)),
                  pl.BlockSpec((n, LANES), lambda i: (i, 0)),
                  pl.BlockSpec(memory_space=pl.ANY)],
        out_specs=pl.BlockSpec((n, D_MODEL), lambda i: (i, 0)),
        out_shape=jax.ShapeDtypeStruct((t, D_MODEL), F32),
        scratch_shapes=[pltpu.VMEM((2, 2 * n, ROW_CHUNKS, LANES), F32),
                        pltpu.SemaphoreType.DMA((2,))],
        compiler_params=_params(("arbitrary",)),
        name="moe_combine",
    )(dest_blocks, dest_blocks, x3, gates, ys)


def _even_weights(w_in, q_norm, k_norm):
    a = 2 * GLA_QK + 2 * GLA_V
    pad = jnp.zeros((D_MODEL, LANES - GLA_RANK), w_in.dtype)
    w = jnp.concatenate([w_in[:, :a], w_in[:, a + GLA_RANK:], w_in[:, a:a + GLA_RANK], pad], axis=1)
    qn = (jnp.tile(q_norm, DIL_HEADS) * DIL_HD ** -0.5).reshape(1, QK_DIL)
    kn = jnp.tile(k_norm, DIL_HEADS).reshape(1, QK_DIL)
    grp = np.zeros((QK_DIL, LANES), np.float32)
    grp[np.arange(QK_DIL), np.arange(QK_DIL) // DIL_HD] = 1.0
    return (w.astype(BF16), qn.astype(F32), kn.astype(F32),
            jnp.asarray(grp, BF16), jnp.asarray(grp.T.copy(), BF16))


def _chunk_cols(w, chunk):
    k, n = w.shape
    return w.reshape(k, n // chunk, chunk).transpose(1, 0, 2)


def kernel(x, mix_norm_even, w_in_even, w_gla_gate_up, b_gla_gate, gla_out_norm, q_norm, k_norm,
           w_out_even, ffn_norm_even, w_ffn_gate, w_ffn_up, w_ffn_down, mix_norm_odd, w_in_odd,
           ret_group_norm, w_out_odd, ffn_norm_odd, w_router, w_exp_gate, w_exp_up, w_exp_down):
    bsz, seq, d = x.shape
    t = bsz * seq
    x2d = x.reshape(t, d)
    row_vec = lambda v: v.reshape(1, -1).astype(F32)

    w_in, qn, kn, grp, grpt = _even_weights(w_in_even[0], q_norm[0], k_norm[0])
    qk_a, v_a, r_a, q_b, k_b, v_b, lr = _in_even(x2d, row_vec(mix_norm_even[0]), w_in, qn, kn, grp, grpt)
    wgu = jnp.zeros((LANES, GLA_QK), F32).at[:GLA_RANK].set(w_gla_gate_up[0]).astype(BF16)
    o_a = _gla(qk_a, v_a, r_a, lr, wgu, row_vec(b_gla_gate[0]), row_vec(gla_out_norm[0]), bsz, seq)
    passes = [_dilated_pass(q_b, k_b, v_b, bsz, seq, dil) for _, dil in DIL_PATTERNS]
    w_out = w_out_even[0].astype(BF16)
    x2d = _out_even(x2d, o_a, [p[0] for p in passes], [p[1] for p in passes],
                    w_out[:GLA_V], w_out[GLA_V:])
    x2d = _ffn(x2d, row_vec(ffn_norm_even[0]),
               _chunk_cols(w_ffn_gate[0].astype(BF16), FFN_CHUNK),
               _chunk_cols(w_ffn_up[0].astype(BF16), FFN_CHUNK),
               w_ffn_down[0].astype(BF16).reshape(FFN_DENSE // FFN_CHUNK, FFN_CHUNK, D_MODEL))

    hcat = _in_odd(x2d, row_vec(mix_norm_odd[0]), w_in_odd[0].astype(BF16))
    o_c = _retention(hcat, row_vec(ret_group_norm[0]), bsz, seq)
    wr = jnp.zeros((D_MODEL, LANES), F32).at[:, :N_EXPERTS].set(w_router[0])
    wr_hi = wr.astype(BF16)
    wr_lo = (wr - wr_hi.astype(F32)).astype(BF16)
    x3, xn3, meta, gates, counts = _out_odd(x2d, o_c, w_out_odd[0].astype(BF16),
                                            row_vec(ffn_norm_odd[0]), wr_hi, wr_lo)

    bm = MOE_ROWS
    n_blocks = t * 2 // bm + N_EXPERTS
    n_rows = n_blocks * bm
    cnt = counts[0, :N_EXPERTS].astype(jnp.int32)
    padded = (cnt + bm - 1) // bm * bm
    pends = jnp.cumsum(padded)
    pstarts = pends - padded
    dest1 = pstarts[meta[:, 0]] + meta[:, 2]
    dest2 = pstarts[meta[:, 1]] + meta[:, 3]
    ng = t // GATHER_ROWS
    dest_blocks = jnp.concatenate([dest1.reshape(ng, GATHER_ROWS), dest2.reshape(ng, GATHER_ROWS)],
                                  axis=1).reshape(ng, 1, 2 * GATHER_ROWS).astype(jnp.int32)
    block_expert = jnp.minimum(
        jnp.searchsorted(pends, jnp.arange(n_blocks, dtype=jnp.int32) * bm, side='right'),
        N_EXPERTS - 1).astype(jnp.int32)
    n_valid = (pends[-1:] // bm).astype(jnp.int32)

    xs = _dispatch(xn3, dest_blocks, n_rows)
    ys = _moe_experts(xs, block_expert, n_valid, w_exp_gate[0].astype(BF16),
                      w_exp_up[0].astype(BF16), w_exp_down[0].astype(BF16))
    out = _combine(x3, gates, ys, dest_blocks)
    return out.reshape(bsz, seq, d)
```

```python
import functools

import numpy as np
import jax
import jax.numpy as jnp
from jax import lax
from jax.experimental import pallas as pl
from jax.experimental.pallas import tpu as pltpu

BF16 = jnp.bfloat16
F32 = jnp.float32

D_MODEL = 1024
NORM_EPS = 1e-6
GN_EPS = 1e-5
NEG_INF = -1e30

GLA_HEADS, GLA_DK, GLA_DV, GLA_RANK, GLA_TAU, GLA_CHUNK = 4, 128, 256, 16, 16.0, 64
DIL_HEADS, DIL_HD, DIL_BLOCK = 8, 64, 128
DIL_PATTERNS = ((128, 1), (512, 4), (2048, 16))
RET_HEADS, RET_DK, RET_DV, RET_CHUNK = 4, 256, 512, 128
FFN_DENSE = 2816
N_EXPERTS, FFN_EXPERT = 8, 3584

LANES = 128
SUBLANES = 8
ROW_CHUNKS = D_MODEL // LANES
VMEM_LIMIT = 56 * 1024 * 1024

ROW_TILE = 512
GLA_STEP = 256
FFN_CHUNK = 256
MOE_ROWS = 512
MOE_FCHUNK = 512
GATHER_ROWS = 256
DISPATCH_ROWS = 1024

QK_DIL = DIL_HEADS * DIL_HD
GLA_QK = GLA_HEADS * GLA_DK
GLA_V = GLA_HEADS * GLA_DV
EVEN_W = 2 * GLA_QK + 2 * GLA_V + 3 * QK_DIL + LANES
ODD_W = 2 * RET_HEADS * RET_DK + 2 * RET_HEADS * RET_DV


def _dot(a, b):
    return jnp.dot(a, b, preferred_element_type=F32)


def _dot_nt(a, b):
    return lax.dot_general(a, b, (((1,), (1,)), ((), ())), preferred_element_type=F32)


def _dot_tn(a, b):
    return lax.dot_general(a, b, (((0,), (0,)), ((), ())), preferred_element_type=F32)


def _split(x):
    hi = x.astype(BF16)
    lo = (x - hi.astype(F32)).astype(BF16)
    return hi, lo


def _rms(x, gain, eps=NORM_EPS):
    ms = jnp.mean(x * x, axis=-1, keepdims=True)
    return x * lax.rsqrt(ms + eps) * gain


def _silu(x):
    return x / (1.0 + jnp.exp(-x))


def _log_sigmoid(z):
    return jnp.minimum(z, 0.0) - jnp.log1p(jnp.exp(-jnp.abs(z)))


def _params(sem):
    return pltpu.CompilerParams(dimension_semantics=sem, vmem_limit_bytes=VMEM_LIMIT)


def _resident(shape):
    nd = len(shape)
    return pl.BlockSpec(shape, lambda *_: (0,) * nd, pipeline_mode=pl.Buffered(1))


def _head_inv_rms(acc, grp_ref, grpt_ref):
    hi, lo = _split(acc * acc)
    ss = _dot(hi, grp_ref[...]) + _dot(lo, grp_ref[...])
    inv = lax.rsqrt(ss * (1.0 / DIL_HD) + NORM_EPS)
    ih, il = _split(inv)
    return _dot(ih, grpt_ref[...]) + _dot(il, grpt_ref[...])


def _in_even_kernel(x_ref, g_ref, w_ref, qn_ref, kn_ref, grp_ref, grpt_ref,
                    qk_ref, v_ref, r_ref, qb_ref, kb_ref, vb_ref, lr_ref):
    xn = _rms(x_ref[...], g_ref[...]).astype(BF16)
    c = 0
    for ref, width in ((qk_ref, 2 * GLA_QK), (v_ref, GLA_V), (r_ref, GLA_V)):
        ref[...] = _dot(xn, w_ref[:, c:c + width]).astype(BF16)
        c += width
    qb = _dot(xn, w_ref[:, c:c + QK_DIL])
    qb_ref[...] = (qb * _head_inv_rms(qb, grp_ref, grpt_ref) * qn_ref[...]).astype(BF16)
    c += QK_DIL
    kb = _dot(xn, w_ref[:, c:c + QK_DIL])
    kb_ref[...] = (kb * _head_inv_rms(kb, grp_ref, grpt_ref) * kn_ref[...]).astype(BF16)
    c += QK_DIL
    vb_ref[...] = _dot(xn, w_ref[:, c:c + QK_DIL]).astype(BF16)
    c += QK_DIL
    lr_ref[...] = _dot(xn, w_ref[:, c:c + LANES])


def _in_even(x2d, gain, w, qn, kn, grp, grpt):
    t = x2d.shape[0]
    tm = ROW_TILE
    row = lambda width: pl.BlockSpec((tm, width), lambda i: (i, 0))
    out_shape = (
        jax.ShapeDtypeStruct((t, 2 * GLA_QK), BF16),
        jax.ShapeDtypeStruct((t, GLA_V), BF16),
        jax.ShapeDtypeStruct((t, GLA_V), BF16),
        jax.ShapeDtypeStruct((t, QK_DIL), BF16),
        jax.ShapeDtypeStruct((t, QK_DIL), BF16),
        jax.ShapeDtypeStruct((t, QK_DIL), BF16),
        jax.ShapeDtypeStruct((t, LANES), F32),
    )
    return pl.pallas_call(
        _in_even_kernel,
        grid=(t // tm,),
        in_specs=[row(D_MODEL), _resident((1, D_MODEL)), _resident((D_MODEL, EVEN_W)),
                  _resident((1, QK_DIL)), _resident((1, QK_DIL)),
                  _resident((QK_DIL, LANES)), _resident((LANES, QK_DIL))],
        out_specs=[row(2 * GLA_QK), row(GLA_V), row(GLA_V), row(QK_DIL), row(QK_DIL),
                   row(QK_DIL), row(LANES)],
        out_shape=out_shape,
        compiler_params=_params(("parallel",)),
        name="in_even",
    )(x2d, gain, w, qn, kn, grp, grpt)


def _gla_kernel(qk_ref, v_ref, r_ref, lr_ref, wgu_ref, bg_ref, on_ref, o_ref, st_ref):
    c = GLA_CHUNK

    @pl.when(pl.program_id(1) == 0)
    def _():
        st_ref[...] = jnp.zeros_like(st_ref)

    rows = lax.broadcasted_iota(jnp.int32, (c, c), 0)
    cols = lax.broadcasted_iota(jnp.int32, (c, c), 1)
    causal = rows >= cols
    tri = causal.astype(BF16)

    def chunk(ci, carry):
        r0 = pl.multiple_of(ci * c, c)
        rs = pl.ds(r0, c)
        z = _dot(lr_ref[rs, :].astype(BF16), wgu_ref[...]) + bg_ref[...]
        log_a = _log_sigmoid(z) * (1.0 / GLA_TAU)
        hi, lo = _split(log_a)
        cum = _dot(tri, hi) + _dot(tri, lo)
        total = cum[c - 1:c, :]
        q = qk_ref[rs, 0:GLA_QK].astype(F32) * GLA_DK ** -0.5
        k = qk_ref[rs, GLA_QK:2 * GLA_QK].astype(F32)
        q_dec = (q * jnp.exp(cum)).astype(BF16)
        k_inv = (k * jnp.exp(-cum)).astype(BF16)
        k_end = (k * jnp.exp(total - cum)).astype(BF16)
        e_tot = jnp.exp(total)
        for h in range(GLA_HEADS):
            ks = slice(h * GLA_DK, (h + 1) * GLA_DK)
            vs = slice(h * GLA_DV, (h + 1) * GLA_DV)
            v_h = v_ref[rs, vs]
            att = jnp.where(causal, _dot_nt(q_dec[:, ks], k_inv[:, ks]), 0.0).astype(BF16)
            st = st_ref[h]
            o = _dot(att, v_h) + _dot_nt(q_dec[:, ks], st.astype(BF16))
            st_ref[h] = st * e_tot[:, ks] + _dot_tn(v_h, k_end[:, ks])
            gate = _silu(r_ref[rs, vs].astype(F32))
            o_ref[rs, vs] = (_rms(o, on_ref[...]) * gate).astype(BF16)
        return carry

    lax.fori_loop(0, GLA_STEP // c, chunk, 0)


def _gla(qk, v, r, lr, wgu, bg, on, bsz, seq):
    t = qk.shape[0]
    ns = seq // GLA_STEP
    row = lambda width: pl.BlockSpec((GLA_STEP, width), lambda b, s: (b * ns + s, 0))
    return pl.pallas_call(
        _gla_kernel,
        grid=(bsz, ns),
        in_specs=[row(2 * GLA_QK), row(GLA_V), row(GLA_V), row(LANES),
                  _resident((LANES, GLA_QK)), _resident((1, GLA_QK)), _resident((1, GLA_DV))],
        out_specs=row(GLA_V),
        out_shape=jax.ShapeDtypeStruct((t, GLA_V), BF16),
        scratch_shapes=[pltpu.VMEM((GLA_HEADS, GLA_DV, GLA_DK), F32)],
        compiler_params=_params(("parallel", "arbitrary")),
        name="gla",
    )(qk, v, r, lr, wgu, bg, on)


def _dil_kernel(q_ref, kc_ref, vc_ref, kp_ref, vp_ref, o_ref, lse_ref, *, dilation):
    blk = DIL_BLOCK
    has_prev = pl.program_id(2) > 0
    rows = lax.broadcasted_iota(jnp.int32, (blk, blk), 0)
    cols = lax.broadcasted_iota(jnp.int32, (blk, blk), 1)
    dist_cur = ((rows - cols) * dilation).astype(F32)
    dist_prev = dist_cur + float(blk * dilation)
    ok_cur = cols <= rows
    ok_prev = (cols >= rows) & has_prev
    for h in range(DIL_HEADS):
        slope = 2.0 ** (-8.0 * (h + 1) / DIL_HEADS)
        hs = slice(h * DIL_HD, (h + 1) * DIL_HD)
        q = q_ref[0, :, hs]
        s_cur = jnp.where(ok_cur, _dot_nt(q, kc_ref[0, :, hs]) - slope * dist_cur, NEG_INF)
        s_prev = jnp.where(ok_prev, _dot_nt(q, kp_ref[0, :, hs]) - slope * dist_prev, NEG_INF)
        m = jnp.maximum(jnp.max(s_cur, axis=-1, keepdims=True),
                        jnp.max(s_prev, axis=-1, keepdims=True))
        p_cur = jnp.exp(s_cur - m)
        p_prev = jnp.exp(s_prev - m)
        l = jnp.sum(p_cur, axis=-1, keepdims=True) + jnp.sum(p_prev, axis=-1, keepdims=True)
        o = (_dot(p_cur.astype(BF16), vc_ref[0, :, hs])
             + _dot(p_prev.astype(BF16), vp_ref[0, :, hs])) / l
        o_ref[0, :, hs] = o.astype(BF16)
        lse_ref[0, :, hs] = jnp.broadcast_to(m + jnp.log(l), (blk, DIL_HD))


def _dilated_pass(qb, kb, vb, bsz, seq, dilation):
    seg = seq // dilation
    nb = seg // DIL_BLOCK
    view = lambda a: a.reshape(bsz, seg, dilation * QK_DIL)
    cur = pl.BlockSpec((1, DIL_BLOCK, QK_DIL), lambda b, r, n: (b, n, r))
    prev = pl.BlockSpec((1, DIL_BLOCK, QK_DIL), lambda b, r, n: (b, jnp.maximum(n - 1, 0), r))
    o, lse = pl.pallas_call(
        functools.partial(_dil_kernel, dilation=dilation),
        grid=(bsz, dilation, nb),
        in_specs=[cur, cur, cur, prev, prev],
        out_specs=[cur, cur],
        out_shape=(jax.ShapeDtypeStruct((bsz, seg, dilation * QK_DIL), BF16),
                   jax.ShapeDtypeStruct((bsz, seg, dilation * QK_DIL), F32)),
        compiler_params=_params(("parallel", "parallel", "arbitrary")),
        name=f"dilated_d{dilation}",
    )(view(qb), view(kb), view(vb), view(kb), view(vb))
    return o.reshape(bsz * seq, QK_DIL), lse.reshape(bsz * seq, QK_DIL)


def _out_even_kernel(x_ref, oa_ref, o1_ref, o2_ref, o3_ref, l1_ref, l2_ref, l3_ref,
                     wa_ref, wb_ref, y_ref):
    l1, l2, l3 = l1_ref[...], l2_ref[...], l3_ref[...]
    m = jnp.maximum(jnp.maximum(l1, l2), l3)
    e1, e2, e3 = jnp.exp(l1 - m), jnp.exp(l2 - m), jnp.exp(l3 - m)
    ob = (e1 * o1_ref[...].astype(F32) + e2 * o2_ref[...].astype(F32)
          + e3 * o3_ref[...].astype(F32)) / (e1 + e2 + e3)
    y_ref[...] = x_ref[...] + _dot(oa_ref[...], wa_ref[...]) + _dot(ob.astype(BF16), wb_ref[...])


def _out_even(x2d, oa, os_, ls, wa, wb):
    t = x2d.shape[0]
    tm = ROW_TILE
    row = lambda width: pl.BlockSpec((tm, width), lambda i: (i, 0))
    return pl.pallas_call(
        _out_even_kernel,
        grid=(t // tm,),
        in_specs=[row(D_MODEL), row(GLA_V)] + [row(QK_DIL)] * 6
                 + [_resident((GLA_V, D_MODEL)), _resident((QK_DIL, D_MODEL))],
        out_specs=row(D_MODEL),
        out_shape=jax.ShapeDtypeStruct((t, D_MODEL), F32),
        compiler_params=_params(("parallel",)),
        name="out_even",
    )(x2d, oa, *os_, *ls, wa, wb)


def _ffn_kernel(x_ref, g_ref, wg_ref, wu_ref, wd_ref, y_ref, xn_ref):
    x = x_ref[...]
    xn_ref[...] = _rms(x, g_ref[...]).astype(BF16)
    y_ref[...] = x

    def chunk(ci, carry):
        xn = xn_ref[...]
        h = _silu(_dot(xn, wg_ref[ci])) * _dot(xn, wu_ref[ci])
        y_ref[...] += _dot(h.astype(BF16), wd_ref[ci])
        return carry

    lax.fori_loop(0, FFN_DENSE // FFN_CHUNK, chunk, 0)


def _ffn(x2d, gain, wg, wu, wd):
    t = x2d.shape[0]
    tm = ROW_TILE
    nc = FFN_DENSE // FFN_CHUNK
    row = pl.BlockSpec((tm, D_MODEL), lambda i: (i, 0))
    return pl.pallas_call(
        _ffn_kernel,
        grid=(t // tm,),
        in_specs=[row, _resident((1, D_MODEL)), _resident((nc, D_MODEL, FFN_CHUNK)),
                  _resident((nc, D_MODEL, FFN_CHUNK)), _resident((nc, FFN_CHUNK, D_MODEL))],
        out_specs=row,
        out_shape=jax.ShapeDtypeStruct((t, D_MODEL), F32),
        scratch_shapes=[pltpu.VMEM((tm, D_MODEL), BF16)],
        compiler_params=_params(("parallel",)),
        name="ffn_dense",
    )(x2d, gain, wg, wu, wd)


def _in_odd_kernel(x_ref, g_ref, w_ref, h_ref):
    xn = _rms(x_ref[...], g_ref[...]).astype(BF16)
    width = RET_HEADS * RET_DK
    for ci in range(ODD_W // width):
        cs = slice(ci * width, (ci + 1) * width)
        acc = _dot(xn, w_ref[:, cs])
        if ci == 1:
            acc = acc * RET_DK ** -0.5
        h_ref[:, cs] = acc.astype(BF16)


def _in_odd(x2d, gain, w):
    t = x2d.shape[0]
    tm = ROW_TILE
    return pl.pallas_call(
        _in_odd_kernel,
        grid=(t // tm,),
        in_specs=[pl.BlockSpec((tm, D_MODEL), lambda i: (i, 0)), _resident((1, D_MODEL)),
                  _resident((D_MODEL, ODD_W))],
        out_specs=pl.BlockSpec((tm, ODD_W), lambda i: (i, 0)),
        out_shape=jax.ShapeDtypeStruct((t, ODD_W), BF16),
        compiler_params=_params(("parallel",)),
        name="in_odd",
    )(x2d, gain, w)


def _ret_kernel(q_ref, k_ref, v_ref, g_ref, dec_ref, qd_ref, kd_ref, cd_ref, gn_ref,
                o_ref, st_ref):
    @pl.when(pl.program_id(2) == 0)
    def _():
        st_ref[...] = jnp.zeros_like(st_ref)

    q = q_ref[...]
    k = k_ref[...]
    v = v_ref[...]
    att = (_dot_nt(q, k) * dec_ref[0]).astype(BF16)
    q_dec = (q.astype(F32) * qd_ref[0]).astype(BF16)
    k_dec = (k.astype(F32) * kd_ref[0]).astype(BF16)
    st = st_ref[...]
    o = _dot(att, v) + _dot(q_dec, st.astype(BF16))
    st_ref[...] = st * cd_ref[0] + _dot_tn(k_dec, v)
    mu = jnp.mean(o, axis=-1, keepdims=True)
    oc = o - mu
    var = jnp.mean(oc * oc, axis=-1, keepdims=True)
    gate = _silu(g_ref[...].astype(F32))
    o_ref[...] = (oc * lax.rsqrt(var + GN_EPS) * gn_ref[...] * gate).astype(BF16)


def _retention_tables():
    h, c = RET_HEADS, RET_CHUNK
    log_gamma = np.log1p(-np.exp2(-5.0 - np.arange(h, dtype=np.float64)))
    pos = np.arange(c, dtype=np.float64)
    rel = pos[:, None] - pos[None, :]
    decay = np.where(rel >= 0, np.exp(log_gamma[:, None, None] * np.maximum(rel, 0.0)), 0.0)
    q_dec = np.exp(log_gamma[:, None] * (pos + 1.0))[:, :, None]
    k_dec = np.exp(log_gamma[:, None] * (c - 1.0 - pos))[:, :, None]
    chunk_decay = np.broadcast_to(np.exp(log_gamma * c)[:, None, None], (h, 1, RET_DV))
    f = lambda a: jnp.asarray(np.ascontiguousarray(a), F32)
    return f(decay), f(q_dec), f(k_dec), f(chunk_decay)


def _retention(hcat, gn_w, bsz, seq):
    t = hcat.shape[0]
    c = RET_CHUNK
    ns = seq // c
    decay, q_dec, k_dec, chunk_decay = _retention_tables()
    nq = RET_HEADS * RET_DK // RET_DK
    nv = 2 * RET_HEADS * RET_DK // RET_DV
    col = lambda width, base: pl.BlockSpec((c, width), lambda b, h, s: (b * ns + s, base + h))
    per_head = lambda shape: pl.BlockSpec((1,) + shape, lambda b, h, s: (h, 0, 0))
    return pl.pallas_call(
        _ret_kernel,
        grid=(bsz, RET_HEADS, ns),
        in_specs=[col(RET_DK, 0), col(RET_DK, nq), col(RET_DV, nv), col(RET_DV, nv + RET_HEADS),
                  per_head((c, c)), per_head((c, 1)), per_head((c, 1)), per_head((1, RET_DV)),
                  pl.BlockSpec((1, RET_DV), lambda b, h, s: (0, h))],
        out_specs=col(RET_DV, 0),
        out_shape=jax.ShapeDtypeStruct((t, RET_HEADS * RET_DV), BF16),
        scratch_shapes=[pltpu.VMEM((RET_DK, RET_DV), F32)],
        compiler_params=_params(("parallel", "parallel", "arbitrary")),
        name="retention",
    )(hcat, hcat, hcat, hcat, decay, q_dec, k_dec, chunk_decay, gn_w)


def _out_odd_kernel(x_ref, oc_ref, w_ref, g_ref, wrh_ref, wrl_ref, tri_ref,
                    y_ref, xn_ref, meta_ref, gate_ref, cnt_ref, run_ref):
    @pl.when(pl.program_id(0) == 0)
    def _():
        run_ref[...] = jnp.zeros_like(run_ref)

    y = x_ref[...] + _dot(oc_ref[...], w_ref[...])
    y_ref[...] = y
    xn = _rms(y, g_ref[...])
    for s in range(ROW_CHUNKS):
        xn_ref[:, s, :] = xn[:, s * LANES:(s + 1) * LANES]

    hi, lo = _split(xn)
    logits = _dot(hi, wrh_ref[...]) + _dot(lo, wrh_ref[...]) + _dot(hi, wrl_ref[...])
    lane = lax.broadcasted_iota(jnp.int32, logits.shape, 1).astype(F32)
    logits = jnp.where(lane < N_EXPERTS, logits, -jnp.inf)
    t1 = jnp.max(logits, axis=-1, keepdims=True)
    i1 = jnp.min(jnp.where(logits == t1, lane, float(LANES)), axis=-1, keepdims=True)
    rest = jnp.where(lane == i1, -jnp.inf, logits)
    t2 = jnp.max(rest, axis=-1, keepdims=True)
    i2 = jnp.min(jnp.where(rest == t2, lane, float(LANES)), axis=-1, keepdims=True)
    e2 = jnp.exp(t2 - t1)
    den = 1.0 + e2
    gate_ref[...] = jnp.where(lane == 0.0, 1.0 / den, jnp.where(lane == 1.0, e2 / den, 0.0))

    sel = (lane == i1) | (lane == i2)
    rank = _dot(tri_ref[...], sel.astype(BF16)) + run_ref[...]
    r1 = jnp.sum(jnp.where(lane == i1, rank, 0.0), axis=-1, keepdims=True)
    r2 = jnp.sum(jnp.where(lane == i2, rank, 0.0), axis=-1, keepdims=True)
    run_ref[...] += jnp.sum(sel.astype(F32), axis=0, keepdims=True)
    cnt_ref[...] = run_ref[...]
    meta = jnp.where(lane == 0.0, i1, jnp.where(lane == 1.0, i2,
                     jnp.where(lane == 2.0, r1, jnp.where(lane == 3.0, r2, 0.0))))
    meta_ref[...] = meta.astype(jnp.int32)


def _out_odd(x2d, oc, w, gain, wr_hi, wr_lo):
    t = x2d.shape[0]
    tm = ROW_TILE
    tri = jnp.asarray(np.tril(np.ones((tm, tm), np.float32), -1), BF16)
    row = lambda width: pl.BlockSpec((tm, width), lambda i: (i, 0))
    return pl.pallas_call(
        _out_odd_kernel,
        grid=(t // tm,),
        in_specs=[row(D_MODEL), row(RET_HEADS * RET_DV), _resident((RET_HEADS * RET_DV, D_MODEL)),
                  _resident((1, D_MODEL)), _resident((D_MODEL, LANES)),
                  _resident((D_MODEL, LANES)), _resident((tm, tm))],
        out_specs=[row(D_MODEL), pl.BlockSpec((tm, ROW_CHUNKS, LANES), lambda i: (i, 0, 0)),
                   row(LANES), row(LANES), pl.BlockSpec((1, LANES), lambda i: (0, 0))],
        out_shape=(jax.ShapeDtypeStruct((t, D_MODEL), F32),
                   jax.ShapeDtypeStruct((t, ROW_CHUNKS, LANES), F32),
                   jax.ShapeDtypeStruct((t, LANES), jnp.int32),
                   jax.ShapeDtypeStruct((t, LANES), F32),
                   jax.ShapeDtypeStruct((1, LANES), F32)),
        scratch_shapes=[pltpu.VMEM((1, LANES), F32)],
        compiler_params=_params(("arbitrary",)),
        name="out_odd_router",
    )(x2d, oc, w, gain, wr_hi, wr_lo, tri)


def _row_copy(src_ref, src_row, dst_ref, dst_row, sem):
    return pltpu.make_async_copy(src_ref.at[src_row], dst_ref.at[dst_row], sem)


def _dispatch_kernel(dest_ref, x_ref, init_ref, xs_ref, sem):
    del init_ref
    n = DISPATCH_ROWS

    def issue(j, carry):
        _row_copy(x_ref, j, xs_ref, dest_ref[0, 0, j], sem).start()
        _row_copy(x_ref, j, xs_ref, dest_ref[0, 0, n + j], sem).start()
        return carry

    lax.fori_loop(0, n, issue, 0, unroll=8)

    def drain(j, carry):
        _row_copy(x_ref, 0, xs_ref, 0, sem).wait()
        return carry

    lax.fori_loop(0, 2 * n, drain, 0, unroll=8)


def _dispatch(xn3, dest_blocks, n_rows):
    t = xn3.shape[0]
    n = DISPATCH_ROWS
    zeros = jnp.zeros((n_rows, ROW_CHUNKS, LANES), F32)
    return pl.pallas_call(
        _dispatch_kernel,
        grid=(t // n,),
        in_specs=[pl.BlockSpec((1, 1, 2 * n), lambda i: (i, 0, 0), memory_space=pltpu.SMEM),
                  pl.BlockSpec((n, ROW_CHUNKS, LANES), lambda i: (i, 0, 0)),
                  pl.BlockSpec(memory_space=pl.ANY)],
        out_specs=pl.BlockSpec(memory_space=pl.ANY),
        out_shape=jax.ShapeDtypeStruct((n_rows, ROW_CHUNKS, LANES), F32),
        scratch_shapes=[pltpu.SemaphoreType.DMA(())],
        input_output_aliases={2: 0},
        compiler_params=_params(("arbitrary",)),
        name="moe_dispatch",
    )(dest_blocks, xn3, zeros)


def _moe_kernel(be_ref, nv_ref, x_ref, wg_ref, wu_ref, wd_ref, y_ref, xb_ref, acc_ref):
    del be_ref
    b = pl.program_id(0)
    f = pl.program_id(1)
    nf = pl.num_programs(1)

    @pl.when(f == 0)
    def _():
        acc_ref[...] = jnp.zeros_like(acc_ref)
        for s in range(ROW_CHUNKS):
            xb_ref[:, s * LANES:(s + 1) * LANES] = x_ref[:, s, :].astype(BF16)

    @pl.when(b < nv_ref[0])
    def _():
        x = xb_ref[...]
        h = _silu(_dot(x, wg_ref[0])) * _dot(x, wu_ref[0])
        acc_ref[...] += _dot(h.astype(BF16), wd_ref[0])

    @pl.when(f == nf - 1)
    def _():
        for s in range(ROW_CHUNKS):
            y_ref[:, s, :] = acc_ref[:, s * LANES:(s + 1) * LANES]


def _moe_experts(xs, block_expert, n_valid, wg, wu, wd):
    n_rows = xs.shape[0]
    bm, tf = MOE_ROWS, MOE_FCHUNK
    nf = FFN_EXPERT // tf
    fidx = lambda b, f, nv: jnp.where(b < nv[0], f, nf - 1)
    rows = pl.BlockSpec((bm, ROW_CHUNKS, LANES), lambda b, f, be, nv: (b, 0, 0))
    grid_spec = pltpu.PrefetchScalarGridSpec(
        num_scalar_prefetch=2,
        grid=(n_rows // bm, nf),
        in_specs=[rows,
                  pl.BlockSpec((1, D_MODEL, tf), lambda b, f, be, nv: (be[b], 0, fidx(b, f, nv))),
                  pl.BlockSpec((1, D_MODEL, tf), lambda b, f, be, nv: (be[b], 0, fidx(b, f, nv))),
                  pl.BlockSpec((1, tf, D_MODEL), lambda b, f, be, nv: (be[b], fidx(b, f, nv), 0))],
        out_specs=rows,
        scratch_shapes=[pltpu.VMEM((bm, D_MODEL), BF16), pltpu.VMEM((bm, D_MODEL), F32)],
    )
    return pl.pallas_call(
        _moe_kernel,
        grid_spec=grid_spec,
        out_shape=jax.ShapeDtypeStruct((n_rows, ROW_CHUNKS, LANES), F32),
        compiler_params=_params(("arbitrary", "arbitrary")),
        name="moe_experts",
    )(block_expert, n_valid, xs, wg, wu, wd)


def _combine_kernel(dcur_ref, dnext_ref, x_ref, gate_ref, y_ref, o_ref, buf_ref, sem):
    n = GATHER_ROWS
    i = pl.program_id(0)
    slot = i % 2

    def issue(dref, sl):
        def body(j, carry):
            _row_copy(y_ref, dref[0, 0, j], buf_ref.at[sl], j, sem.at[sl]).start()
            return carry
        lax.fori_loop(0, 2 * n, body, 0, unroll=8)

    @pl.when(i == 0)
    def _():
        issue(dcur_ref, 0)

    @pl.when(i + 1 < pl.num_programs(0))
    def _():
        issue(dnext_ref, 1 - slot)

    def drain(j, carry):
        _row_copy(y_ref, 0, buf_ref.at[slot], 0, sem.at[slot]).wait()
        return carry

    lax.fori_loop(0, 2 * n, drain, 0, unroll=8)

    g1 = gate_ref[:, 0:1]
    g2 = gate_ref[:, 1:2]
    for s in range(ROW_CHUNKS):
        cs = slice(s * LANES, (s + 1) * LANES)
        o_ref[:, cs] = (x_ref[:, cs] + g1 * buf_ref[slot, 0:n, s, :]
                        + g2 * buf_ref[slot, n:2 * n, s, :])


def _combine(x3, gates, ys, dest_blocks):
    t = x3.shape[0]
    n = GATHER_ROWS
    nt = t // n
    idx = lambda f: pl.BlockSpec((1, 1, 2 * n), f, memory_space=pltpu.SMEM)
    return pl.pallas_call(
        _combine_kernel,
        grid=(nt,),
        in_specs=[idx(lambda i: (i, 0, 0)), idx(lambda i: (jnp.minimum(i + 1, nt - 1), 0, 0)),
                  pl.BlockSpec((n, D_MODEL), lambda i: (i, 0)),
                  pl.BlockSpec((n, LANES), lambda i: (i, 0)),
                  pl.BlockSpec(memory_space=pl.ANY)],
        out_specs=pl.BlockSpec((n, D_MODEL), lambda i: (i, 0)),
        out_shape=jax.ShapeDtypeStruct((t, D_MODEL), F32),
        scratch_shapes=[pltpu.VMEM((2, 2 * n, ROW_CHUNKS, LANES), F32),
                        pltpu.SemaphoreType.DMA((2,))],
        compiler_params=_params(("arbitrary",)),
        name="moe_combine",
    )(dest_blocks, dest_blocks, x3, gates, ys)


def _even_weights(w_in, q_norm, k_norm):
    a = 2 * GLA_QK + 2 * GLA_V
    pad = jnp.zeros((D_MODEL, LANES - GLA_RANK), w_in.dtype)
    w = jnp.concatenate([w_in[:, :a], w_in[:, a + GLA_RANK:], w_in[:, a:a + GLA_RANK], pad], axis=1)
    qn = (jnp.tile(q_norm, DIL_HEADS) * DIL_HD ** -0.5).reshape(1, QK_DIL)
    kn = jnp.tile(k_norm, DIL_HEADS).reshape(1, QK_DIL)
    grp = np.zeros((QK_DIL, LANES), np.float32)
    grp[np.arange(QK_DIL), np.arange(QK_DIL) // DIL_HD] = 1.0
    return (w.astype(BF16), qn.astype(F32), kn.astype(F32),
            jnp.asarray(grp, BF16), jnp.asarray(grp.T.copy(), BF16))


def _chunk_cols(w, chunk):
    k, n = w.shape
    return w.reshape(k, n // chunk, chunk).transpose(1, 0, 2)


def kernel(x, mix_norm_even, w_in_even, w_gla_gate_up, b_gla_gate, gla_out_norm, q_norm, k_norm,
           w_out_even, ffn_norm_even, w_ffn_gate, w_ffn_up, w_ffn_down, mix_norm_odd, w_in_odd,
           ret_group_norm, w_out_odd, ffn_norm_odd, w_router, w_exp_gate, w_exp_up, w_exp_down):
    bsz, seq, d = x.shape
    t = bsz * seq
    x2d = x.reshape(t, d)
    row_vec = lambda v: v.reshape(1, -1).astype(F32)

    w_in, qn, kn, grp, grpt = _even_weights(w_in_even[0], q_norm[0], k_norm[0])
    qk_a, v_a, r_a, q_b, k_b, v_b, lr = _in_even(x2d, row_vec(mix_norm_even[0]), w_in, qn, kn, grp, grpt)
    wgu = jnp.zeros((LANES, GLA_QK), F32).at[:GLA_RANK].set(w_gla_gate_up[0]).astype(BF16)
    o_a = _gla(qk_a, v_a, r_a, lr, wgu, row_vec(b_gla_gate[0]), row_vec(gla_out_norm[0]), bsz, seq)
    passes = [_dilated_pass(q_b, k_b, v_b, bsz, seq, dil) for _, dil in DIL_PATTERNS]
    w_out = w_out_even[0].astype(BF16)
    x2d = _out_even(x2d, o_a, [p[0] for p in passes], [p[1] for p in passes],
                    w_out[:GLA_V], w_out[GLA_V:])
    x2d = _ffn(x2d, row_vec(ffn_norm_even[0]),
               _chunk_cols(w_ffn_gate[0].astype(BF16), FFN_CHUNK),
               _chunk_cols(w_ffn_up[0].astype(BF16), FFN_CHUNK),
               w_ffn_down[0].astype(BF16).reshape(FFN_DENSE // FFN_CHUNK, FFN_CHUNK, D_MODEL))

    hcat = _in_odd(x2d, row_vec(mix_norm_odd[0]), w_in_odd[0].astype(BF16))
    o_c = _retention(hcat, row_vec(ret_group_norm[0]), bsz, seq)
    wr = jnp.zeros((D_MODEL, LANES), F32).at[:, :N_EXPERTS].set(w_router[0])
    wr_hi = wr.astype(BF16)
    wr_lo = (wr - wr_hi.astype(F32)).astype(BF16)
    x3, xn3, meta, gates, counts = _out_odd(x2d, o_c, w_out_odd[0].astype(BF16),
                                            row_vec(ffn_norm_odd[0]), wr_hi, wr_lo)

    bm = MOE_ROWS
    n_blocks = t * 2 // bm + N_EXPERTS
    n_rows = n_blocks * bm
    cnt = counts[0, :N_EXPERTS].astype(jnp.int32)
    padded = (cnt + bm - 1) // bm * bm
    pends = jnp.cumsum(padded)
    pstarts = pends - padded
    dest1 = pstarts[meta[:, 0]] + meta[:, 2]
    dest2 = pstarts[meta[:, 1]] + meta[:, 3]

    def dest_blocks(n):
        return jnp.concatenate([dest1.reshape(t // n, n), dest2.reshape(t // n, n)],
                               axis=1).reshape(t // n, 1, 2 * n).astype(jnp.int32)

    block_expert = jnp.minimum(
        jnp.searchsorted(pends, jnp.arange(n_blocks, dtype=jnp.int32) * bm, side='right'),
        N_EXPERTS - 1).astype(jnp.int32)
    n_valid = (pends[-1:] // bm).astype(jnp.int32)

    xs = _dispatch(xn3, dest_blocks(DISPATCH_ROWS), n_rows)
    ys = _moe_experts(xs, block_expert, n_valid, w_exp_gate[0].astype(BF16),
                      w_exp_up[0].astype(BF16), w_exp_down[0].astype(BF16))
    out = _combine(x3, gates, ys, dest_blocks(GATHER_ROWS))
    return out.reshape(bsz, seq, d)
```

```python
import functools

import numpy as np
import jax
import jax.numpy as jnp
from jax import lax
from jax.experimental import pallas as pl
from jax.experimental.pallas import tpu as pltpu

BF16 = jnp.bfloat16
F32 = jnp.float32

D_MODEL = 1024
NORM_EPS = 1e-6
GN_EPS = 1e-5
NEG_INF = -1e30

GLA_HEADS, GLA_DK, GLA_DV, GLA_RANK, GLA_TAU, GLA_CHUNK = 4, 128, 256, 16, 16.0, 64
DIL_HEADS, DIL_HD, DIL_BLOCK = 8, 64, 128
DIL_PATTERNS = ((128, 1), (512, 4), (2048, 16))
RET_HEADS, RET_DK, RET_DV = 4, 256, 512
FFN_DENSE = 2816
N_EXPERTS, FFN_EXPERT = 8, 3584

LANES = 128
SUBLANES = 8
ROW_CHUNKS = D_MODEL // LANES
VMEM_LIMIT = 56 * 1024 * 1024

ROW_TILE = 512
GLA_STEP = 256
RET_CHUNK = 256
FFN_CHUNK = 256
MOE_ROWS = 512
MOE_FCHUNK = 512
GATHER_ROWS = 256
DISPATCH_ROWS = 1024

QK_DIL = DIL_HEADS * DIL_HD
GLA_QK = GLA_HEADS * GLA_DK
GLA_V = GLA_HEADS * GLA_DV
EVEN_W = 2 * GLA_QK + 2 * GLA_V + 3 * QK_DIL + LANES
ODD_W = 2 * RET_HEADS * RET_DK + 2 * RET_HEADS * RET_DV


def _dot(a, b):
    return jnp.dot(a, b, preferred_element_type=F32)


def _dot_nt(a, b):
    return lax.dot_general(a, b, (((1,), (1,)), ((), ())), preferred_element_type=F32)


def _dot_tn(a, b):
    return lax.dot_general(a, b, (((0,), (0,)), ((), ())), preferred_element_type=F32)


def _split(x):
    hi = x.astype(BF16)
    lo = (x - hi.astype(F32)).astype(BF16)
    return hi, lo


def _rms(x, gain, eps=NORM_EPS):
    ms = jnp.mean(x * x, axis=-1, keepdims=True)
    return x * lax.rsqrt(ms + eps) * gain


def _silu(x):
    return x / (1.0 + jnp.exp(-x))


def _log_sigmoid(z):
    return jnp.minimum(z, 0.0) - jnp.log1p(jnp.exp(-jnp.abs(z)))


def _params(sem):
    return pltpu.CompilerParams(dimension_semantics=sem, vmem_limit_bytes=VMEM_LIMIT)


def _resident(shape):
    nd = len(shape)
    return pl.BlockSpec(shape, lambda *_: (0,) * nd, pipeline_mode=pl.Buffered(1))


def _head_inv_rms(acc, grp_ref, grpt_ref):
    ss = _dot((acc * acc).astype(BF16), grp_ref[...])
    inv = lax.rsqrt(ss * (1.0 / DIL_HD) + NORM_EPS)
    ih, il = _split(inv)
    return _dot(ih, grpt_ref[...]) + _dot(il, grpt_ref[...])


def _in_even_kernel(x_ref, g_ref, w_ref, qn_ref, kn_ref, grp_ref, grpt_ref,
                    qk_ref, v_ref, r_ref, qb_ref, kb_ref, vb_ref, lr_ref):
    xn = _rms(x_ref[...], g_ref[...]).astype(BF16)
    c = 0
    for ref, width in ((qk_ref, 2 * GLA_QK), (v_ref, GLA_V), (r_ref, GLA_V)):
        ref[...] = _dot(xn, w_ref[:, c:c + width]).astype(BF16)
        c += width
    qb = _dot(xn, w_ref[:, c:c + QK_DIL])
    qb_ref[...] = (qb * _head_inv_rms(qb, grp_ref, grpt_ref) * qn_ref[...]).astype(BF16)
    c += QK_DIL
    kb = _dot(xn, w_ref[:, c:c + QK_DIL])
    kb_ref[...] = (kb * _head_inv_rms(kb, grp_ref, grpt_ref) * kn_ref[...]).astype(BF16)
    c += QK_DIL
    vb_ref[...] = _dot(xn, w_ref[:, c:c + QK_DIL]).astype(BF16)
    c += QK_DIL
    lr_ref[...] = _dot(xn, w_ref[:, c:c + LANES])


def _in_even(x2d, gain, w, qn, kn, grp, grpt):
    t = x2d.shape[0]
    tm = ROW_TILE
    row = lambda width: pl.BlockSpec((tm, width), lambda i: (i, 0))
    out_shape = (
        jax.ShapeDtypeStruct((t, 2 * GLA_QK), BF16),
        jax.ShapeDtypeStruct((t, GLA_V), BF16),
        jax.ShapeDtypeStruct((t, GLA_V), BF16),
        jax.ShapeDtypeStruct((t, QK_DIL), BF16),
        jax.ShapeDtypeStruct((t, QK_DIL), BF16),
        jax.ShapeDtypeStruct((t, QK_DIL), BF16),
        jax.ShapeDtypeStruct((t, LANES), F32),
    )
    return pl.pallas_call(
        _in_even_kernel,
        grid=(t // tm,),
        in_specs=[row(D_MODEL), _resident((1, D_MODEL)), _resident((D_MODEL, EVEN_W)),
                  _resident((1, QK_DIL)), _resident((1, QK_DIL)),
                  _resident((QK_DIL, LANES)), _resident((LANES, QK_DIL))],
        out_specs=[row(2 * GLA_QK), row(GLA_V), row(GLA_V), row(QK_DIL), row(QK_DIL),
                   row(QK_DIL), row(LANES)],
        out_shape=out_shape,
        compiler_params=_params(("parallel",)),
        name="in_even",
    )(x2d, gain, w, qn, kn, grp, grpt)


def _gla_kernel(qk_ref, v_ref, r_ref, lr_ref, wgu_ref, bg_ref, on_ref, o_ref, st_ref):
    c = GLA_CHUNK

    @pl.when(pl.program_id(1) == 0)
    def _():
        st_ref[...] = jnp.zeros_like(st_ref)

    rows = lax.broadcasted_iota(jnp.int32, (c, c), 0)
    cols = lax.broadcasted_iota(jnp.int32, (c, c), 1)
    causal = rows >= cols
    tri = causal.astype(BF16)

    def chunk(ci, carry):
        r0 = pl.multiple_of(ci * c, c)
        rs = pl.ds(r0, c)
        z = _dot(lr_ref[rs, :].astype(BF16), wgu_ref[...]) + bg_ref[...]
        log_a = _log_sigmoid(z) * (1.0 / GLA_TAU)
        hi, lo = _split(log_a)
        cum = _dot(tri, hi) + _dot(tri, lo)
        total = cum[c - 1:c, :]
        q = qk_ref[rs, 0:GLA_QK].astype(F32) * GLA_DK ** -0.5
        k = qk_ref[rs, GLA_QK:2 * GLA_QK].astype(F32)
        q_dec = (q * jnp.exp(cum)).astype(BF16)
        k_inv = (k * jnp.exp(-cum)).astype(BF16)
        k_end = (k * jnp.exp(total - cum)).astype(BF16)
        e_tot = jnp.exp(total)
        for h in range(GLA_HEADS):
            ks = slice(h * GLA_DK, (h + 1) * GLA_DK)
            vs = slice(h * GLA_DV, (h + 1) * GLA_DV)
            v_h = v_ref[rs, vs]
            att = jnp.where(causal, _dot_nt(q_dec[:, ks], k_inv[:, ks]), 0.0).astype(BF16)
            st = st_ref[h]
            o = _dot(att, v_h) + _dot_nt(q_dec[:, ks], st.astype(BF16))
            st_ref[h] = st * e_tot[:, ks] + _dot_tn(v_h, k_end[:, ks])
            gate = _silu(r_ref[rs, vs].astype(F32))
            o_ref[rs, vs] = (_rms(o, on_ref[...]) * gate).astype(BF16)
        return carry

    lax.fori_loop(0, GLA_STEP // c, chunk, 0, unroll=True)


def _gla(qk, v, r, lr, wgu, bg, on, bsz, seq):
    t = qk.shape[0]
    ns = seq // GLA_STEP
    row = lambda width: pl.BlockSpec((GLA_STEP, width), lambda b, s: (b * ns + s, 0))
    return pl.pallas_call(
        _gla_kernel,
        grid=(bsz, ns),
        in_specs=[row(2 * GLA_QK), row(GLA_V), row(GLA_V), row(LANES),
                  _resident((LANES, GLA_QK)), _resident((1, GLA_QK)), _resident((1, GLA_DV))],
        out_specs=row(GLA_V),
        out_shape=jax.ShapeDtypeStruct((t, GLA_V), BF16),
        scratch_shapes=[pltpu.VMEM((GLA_HEADS, GLA_DV, GLA_DK), F32)],
        compiler_params=_params(("parallel", "arbitrary")),
        name="gla",
    )(qk, v, r, lr, wgu, bg, on)


def _dil_kernel(q_ref, kc_ref, vc_ref, kp_ref, vp_ref, o_ref, lse_ref, *, dilation):
    blk = DIL_BLOCK
    has_prev = pl.program_id(2) > 0
    rows = lax.broadcasted_iota(jnp.int32, (blk, blk), 0)
    cols = lax.broadcasted_iota(jnp.int32, (blk, blk), 1)
    dist_cur = ((rows - cols) * dilation).astype(F32)
    dist_prev = dist_cur + float(blk * dilation)
    ok_cur = cols <= rows
    ok_prev = (cols >= rows) & has_prev
    for h in range(DIL_HEADS):
        slope = 2.0 ** (-8.0 * (h + 1) / DIL_HEADS)
        hs = slice(h * DIL_HD, (h + 1) * DIL_HD)
        q = q_ref[0, :, hs]
        s_cur = jnp.where(ok_cur, _dot_nt(q, kc_ref[0, :, hs]) - slope * dist_cur, NEG_INF)
        s_prev = jnp.where(ok_prev, _dot_nt(q, kp_ref[0, :, hs]) - slope * dist_prev, NEG_INF)
        m = jnp.maximum(jnp.max(s_cur, axis=-1, keepdims=True),
                        jnp.max(s_prev, axis=-1, keepdims=True))
        p_cur = jnp.exp(s_cur - m)
        p_prev = jnp.exp(s_prev - m)
        l = jnp.sum(p_cur, axis=-1, keepdims=True) + jnp.sum(p_prev, axis=-1, keepdims=True)
        o = (_dot(p_cur.astype(BF16), vc_ref[0, :, hs])
             + _dot(p_prev.astype(BF16), vp_ref[0, :, hs])) / l
        o_ref[0, :, hs] = o.astype(BF16)
        lse_ref[0, :, hs] = jnp.broadcast_to(m + jnp.log(l), (blk, DIL_HD))


def _dilated_pass(qb, kb, vb, bsz, seq, dilation):
    seg = seq // dilation
    nb = seg // DIL_BLOCK
    view = lambda a: a.reshape(bsz, seg, dilation * QK_DIL)
    cur = pl.BlockSpec((1, DIL_BLOCK, QK_DIL), lambda b, r, n: (b, n, r))
    prev = pl.BlockSpec((1, DIL_BLOCK, QK_DIL), lambda b, r, n: (b, jnp.maximum(n - 1, 0), r))
    o, lse = pl.pallas_call(
        functools.partial(_dil_kernel, dilation=dilation),
        grid=(bsz, dilation, nb),
        in_specs=[cur, cur, cur, prev, prev],
        out_specs=[cur, cur],
        out_shape=(jax.ShapeDtypeStruct((bsz, seg, dilation * QK_DIL), BF16),
                   jax.ShapeDtypeStruct((bsz, seg, dilation * QK_DIL), F32)),
        compiler_params=_params(("parallel", "parallel", "arbitrary")),
        name=f"dilated_d{dilation}",
    )(view(qb), view(kb), view(vb), view(kb), view(vb))
    return o.reshape(bsz * seq, QK_DIL), lse.reshape(bsz * seq, QK_DIL)


def _out_even_kernel(x_ref, oa_ref, o1_ref, o2_ref, o3_ref, l1_ref, l2_ref, l3_ref,
                     wa_ref, wb_ref, y_ref):
    l1, l2, l3 = l1_ref[...], l2_ref[...], l3_ref[...]
    m = jnp.maximum(jnp.maximum(l1, l2), l3)
    e1, e2, e3 = jnp.exp(l1 - m), jnp.exp(l2 - m), jnp.exp(l3 - m)
    ob = (e1 * o1_ref[...].astype(F32) + e2 * o2_ref[...].astype(F32)
          + e3 * o3_ref[...].astype(F32)) / (e1 + e2 + e3)
    y_ref[...] = x_ref[...] + _dot(oa_ref[...], wa_ref[...]) + _dot(ob.astype(BF16), wb_ref[...])


def _out_even(x2d, oa, os_, ls, wa, wb):
    t = x2d.shape[0]
    tm = ROW_TILE
    row = lambda width: pl.BlockSpec((tm, width), lambda i: (i, 0))
    return pl.pallas_call(
        _out_even_kernel,
        grid=(t // tm,),
        in_specs=[row(D_MODEL), row(GLA_V)] + [row(QK_DIL)] * 6
                 + [_resident((GLA_V, D_MODEL)), _resident((QK_DIL, D_MODEL))],
        out_specs=row(D_MODEL),
        out_shape=jax.ShapeDtypeStruct((t, D_MODEL), F32),
        compiler_params=_params(("parallel",)),
        name="out_even",
    )(x2d, oa, *os_, *ls, wa, wb)


def _ffn_kernel(x_ref, g_ref, wg_ref, wu_ref, wd_ref, y_ref, xn_ref, h_ref):
    x = x_ref[...]
    xn_ref[...] = _rms(x, g_ref[...]).astype(BF16)
    for ci in range(FFN_DENSE // FFN_CHUNK):
        cs = slice(ci * FFN_CHUNK, (ci + 1) * FFN_CHUNK)
        xn = xn_ref[...]
        h_ref[:, cs] = (_silu(_dot(xn, wg_ref[:, cs])) * _dot(xn, wu_ref[:, cs])).astype(BF16)
    y_ref[...] = x + _dot(h_ref[...], wd_ref[...])


def _ffn(x2d, gain, wg, wu, wd):
    t = x2d.shape[0]
    tm = ROW_TILE
    row = pl.BlockSpec((tm, D_MODEL), lambda i: (i, 0))
    return pl.pallas_call(
        _ffn_kernel,
        grid=(t // tm,),
        in_specs=[row, _resident((1, D_MODEL)), _resident((D_MODEL, FFN_DENSE)),
                  _resident((D_MODEL, FFN_DENSE)), _resident((FFN_DENSE, D_MODEL))],
        out_specs=row,
        out_shape=jax.ShapeDtypeStruct((t, D_MODEL), F32),
        scratch_shapes=[pltpu.VMEM((tm, D_MODEL), BF16), pltpu.VMEM((tm, FFN_DENSE), BF16)],
        compiler_params=_params(("parallel",)),
        name="ffn_dense",
    )(x2d, gain, wg, wu, wd)


def _in_odd_kernel(x_ref, g_ref, w_ref, h_ref):
    xn = _rms(x_ref[...], g_ref[...]).astype(BF16)
    width = RET_HEADS * RET_DK
    for ci in range(ODD_W // width):
        cs = slice(ci * width, (ci + 1) * width)
        acc = _dot(xn, w_ref[:, cs])
        if ci == 1:
            acc = acc * RET_DK ** -0.5
        h_ref[:, cs] = acc.astype(BF16)


def _in_odd(x2d, gain, w):
    t = x2d.shape[0]
    tm = ROW_TILE
    return pl.pallas_call(
        _in_odd_kernel,
        grid=(t // tm,),
        in_specs=[pl.BlockSpec((tm, D_MODEL), lambda i: (i, 0)), _resident((1, D_MODEL)),
                  _resident((D_MODEL, ODD_W))],
        out_specs=pl.BlockSpec((tm, ODD_W), lambda i: (i, 0)),
        out_shape=jax.ShapeDtypeStruct((t, ODD_W), BF16),
        compiler_params=_params(("parallel",)),
        name="in_odd",
    )(x2d, gain, w)


def _ret_kernel(h_ref, dec_ref, qd_ref, kd_ref, cd_ref, gn_ref, o_ref, st_ref):
    @pl.when(pl.program_id(1) == 0)
    def _():
        st_ref[...] = jnp.zeros_like(st_ref)

    nk = RET_HEADS * RET_DK
    for h in range(RET_HEADS):
        ks = slice(h * RET_DK, (h + 1) * RET_DK)
        vs = slice(h * RET_DV, (h + 1) * RET_DV)
        q = h_ref[:, ks]
        k = h_ref[:, nk + h * RET_DK:nk + (h + 1) * RET_DK]
        v = h_ref[:, 2 * nk + h * RET_DV:2 * nk + (h + 1) * RET_DV]
        g = h_ref[:, 2 * nk + (RET_HEADS + h) * RET_DV:2 * nk + (RET_HEADS + h + 1) * RET_DV]
        att = (_dot_nt(q, k) * dec_ref[h]).astype(BF16)
        q_dec = (q.astype(F32) * qd_ref[h]).astype(BF16)
        k_dec = (k.astype(F32) * kd_ref[h]).astype(BF16)
        st = st_ref[h]
        o = _dot(att, v) + _dot(q_dec, st.astype(BF16))
        st_ref[h] = st * cd_ref[h] + _dot_tn(k_dec, v)
        mu = jnp.mean(o, axis=-1, keepdims=True)
        oc = o - mu
        var = jnp.mean(oc * oc, axis=-1, keepdims=True)
        gate = _silu(g.astype(F32))
        o_ref[:, vs] = (oc * lax.rsqrt(var + GN_EPS) * gn_ref[:, vs] * gate).astype(BF16)


def _retention_tables():
    h, c = RET_HEADS, RET_CHUNK
    log_gamma = np.log1p(-np.exp2(-5.0 - np.arange(h, dtype=np.float64)))
    pos = np.arange(c, dtype=np.float64)
    rel = pos[:, None] - pos[None, :]
    decay = np.where(rel >= 0, np.exp(log_gamma[:, None, None] * np.maximum(rel, 0.0)), 0.0)
    q_dec = np.exp(log_gamma[:, None] * (pos + 1.0))[:, :, None]
    k_dec = np.exp(log_gamma[:, None] * (c - 1.0 - pos))[:, :, None]
    chunk_decay = np.broadcast_to(np.exp(log_gamma * c)[:, None, None], (h, 1, RET_DV))
    f = lambda a: jnp.asarray(np.ascontiguousarray(a), F32)
    return f(decay), f(q_dec), f(k_dec), f(chunk_decay)


def _retention(hcat, gn_w, bsz, seq):
    t = hcat.shape[0]
    c = RET_CHUNK
    ns = seq // c
    decay, q_dec, k_dec, chunk_decay = _retention_tables()
    return pl.pallas_call(
        _ret_kernel,
        grid=(bsz, ns),
        in_specs=[pl.BlockSpec((c, ODD_W), lambda b, s: (b * ns + s, 0)),
                  _resident((RET_HEADS, c, c)), _resident((RET_HEADS, c, 1)),
                  _resident((RET_HEADS, c, 1)), _resident((RET_HEADS, 1, RET_DV)),
                  _resident((1, RET_HEADS * RET_DV))],
        out_specs=pl.BlockSpec((c, RET_HEADS * RET_DV), lambda b, s: (b * ns + s, 0)),
        out_shape=jax.ShapeDtypeStruct((t, RET_HEADS * RET_DV), BF16),
        scratch_shapes=[pltpu.VMEM((RET_HEADS, RET_DK, RET_DV), F32)],
        compiler_params=_params(("parallel", "arbitrary")),
        name="retention",
    )(hcat, decay, q_dec, k_dec, chunk_decay, gn_w)


def _out_odd_kernel(x_ref, oc_ref, w_ref, g_ref, wr_ref, tri_ref,
                    y_ref, xn_ref, meta_ref, gate_ref, cnt_ref, run_ref):
    @pl.when(pl.program_id(0) == 0)
    def _():
        run_ref[...] = jnp.zeros_like(run_ref)

    y = x_ref[...] + _dot(oc_ref[...], w_ref[...])
    y_ref[...] = y
    xn = _rms(y, g_ref[...])
    for s in range(ROW_CHUNKS):
        xn_ref[_chunk_rows(s, ROW_TILE), :] = xn[:, s * LANES:(s + 1) * LANES]

    hi, lo = _split(xn)
    both = _dot(hi, wr_ref[...])
    logits = both[:, :LANES] + both[:, LANES:] + _dot(lo, wr_ref[:, :LANES])
    lane = lax.broadcasted_iota(jnp.int32, logits.shape, 1).astype(F32)
    logits = jnp.where(lane < N_EXPERTS, logits, -jnp.inf)
    t1 = jnp.max(logits, axis=-1, keepdims=True)
    i1 = jnp.min(jnp.where(logits == t1, lane, float(LANES)), axis=-1, keepdims=True)
    rest = jnp.where(lane == i1, -jnp.inf, logits)
    t2 = jnp.max(rest, axis=-1, keepdims=True)
    i2 = jnp.min(jnp.where(rest == t2, lane, float(LANES)), axis=-1, keepdims=True)
    e2 = jnp.exp(t2 - t1)
    den = 1.0 + e2
    gate_ref[...] = jnp.where(lane == 0.0, 1.0 / den, jnp.where(lane == 1.0, e2 / den, 0.0))

    sel = (lane == i1) | (lane == i2)
    rank = _dot(tri_ref[...], sel.astype(BF16)) + run_ref[...]
    r1 = jnp.sum(jnp.where(lane == i1, rank, 0.0), axis=-1, keepdims=True)
    r2 = jnp.sum(jnp.where(lane == i2, rank, 0.0), axis=-1, keepdims=True)
    run_ref[...] += jnp.sum(sel.astype(F32), axis=0, keepdims=True)
    cnt_ref[...] = run_ref[...]
    meta = jnp.where(lane == 0.0, i1, jnp.where(lane == 1.0, i2,
                     jnp.where(lane == 2.0, r1, jnp.where(lane == 3.0, r2, 0.0))))
    meta_ref[...] = meta.astype(jnp.int32)


def _out_odd(x2d, oc, w, gain, wr):
    t = x2d.shape[0]
    tm = ROW_TILE
    tri = jnp.asarray(np.tril(np.ones((tm, tm), np.float32), -1), BF16)
    row = lambda width: pl.BlockSpec((tm, width), lambda i: (i, 0))
    return pl.pallas_call(
        _out_odd_kernel,
        grid=(t // tm,),
        in_specs=[row(D_MODEL), row(RET_HEADS * RET_DV), _resident((RET_HEADS * RET_DV, D_MODEL)),
                  _resident((1, D_MODEL)), _resident((D_MODEL, 2 * LANES)), _resident((tm, tm))],
        out_specs=[row(D_MODEL), pl.BlockSpec((tm * ROW_CHUNKS, LANES), lambda i: (i, 0)),
                   row(LANES), row(LANES), pl.BlockSpec((1, LANES), lambda i: (0, 0))],
        out_shape=(jax.ShapeDtypeStruct((t, D_MODEL), F32),
                   jax.ShapeDtypeStruct((t * ROW_CHUNKS, LANES), F32),
                   jax.ShapeDtypeStruct((t, LANES), jnp.int32),
                   jax.ShapeDtypeStruct((t, LANES), F32),
                   jax.ShapeDtypeStruct((1, LANES), F32)),
        scratch_shapes=[pltpu.VMEM((1, LANES), F32)],
        compiler_params=_params(("arbitrary",)),
        name="out_odd_router",
    )(x2d, oc, w, gain, wr, tri)


def _chunk_rows(s, n_rows):
    return pl.ds(s, n_rows, stride=ROW_CHUNKS)


def _tile_row(ref, row):
    return ref.at[pl.ds(pl.multiple_of(row * ROW_CHUNKS, ROW_CHUNKS), ROW_CHUNKS), :]


def _row_copy(src_ref, src_row, dst_ref, dst_row, sem):
    return pltpu.make_async_copy(_tile_row(src_ref, src_row), _tile_row(dst_ref, dst_row), sem)


def _dispatch_kernel(dest_ref, x_ref, init_ref, xs_ref, sem):
    del init_ref
    n = DISPATCH_ROWS

    def issue(j, carry):
        _row_copy(x_ref, j, xs_ref, dest_ref[0, 0, j], sem).start()
        _row_copy(x_ref, j, xs_ref, dest_ref[0, 0, n + j], sem).start()
        return carry

    lax.fori_loop(0, n, issue, 0, unroll=8)

    def drain(j, carry):
        _row_copy(x_ref, 0, xs_ref, 0, sem).wait()
        return carry

    lax.fori_loop(0, 2 * n, drain, 0, unroll=8)


def _dispatch(xn3, dest_blocks, n_rows):
    t = xn3.shape[0] // ROW_CHUNKS
    n = DISPATCH_ROWS
    zeros = jnp.zeros((n_rows * ROW_CHUNKS, LANES), F32)
    return pl.pallas_call(
        _dispatch_kernel,
        grid=(t // n,),
        in_specs=[pl.BlockSpec((1, 1, 2 * n), lambda i: (i, 0, 0), memory_space=pltpu.SMEM),
                  pl.BlockSpec((n * ROW_CHUNKS, LANES), lambda i: (i, 0)),
                  pl.BlockSpec(memory_space=pl.ANY)],
        out_specs=pl.BlockSpec(memory_space=pl.ANY),
        out_shape=jax.ShapeDtypeStruct((n_rows * ROW_CHUNKS, LANES), F32),
        scratch_shapes=[pltpu.SemaphoreType.DMA(())],
        input_output_aliases={2: 0},
        compiler_params=_params(("arbitrary",)),
        name="moe_dispatch",
    )(dest_blocks, xn3, zeros)


def _moe_kernel(be_ref, nv_ref, x_ref, wg_ref, wu_ref, wd_ref, y_ref, xb_ref, acc_ref):
    del be_ref
    b = pl.program_id(0)
    f = pl.program_id(1)
    nf = pl.num_programs(1)

    @pl.when(f == 0)
    def _():
        acc_ref[...] = jnp.zeros_like(acc_ref)
        for s in range(ROW_CHUNKS):
            xb_ref[:, s * LANES:(s + 1) * LANES] = x_ref[_chunk_rows(s, MOE_ROWS), :].astype(BF16)

    @pl.when(b < nv_ref[0])
    def _():
        x = xb_ref[...]
        h = _silu(_dot(x, wg_ref[0])) * _dot(x, wu_ref[0])
        acc_ref[...] += _dot(h.astype(BF16), wd_ref[0])

    @pl.when(f == nf - 1)
    def _():
        for s in range(ROW_CHUNKS):
            y_ref[_chunk_rows(s, MOE_ROWS), :] = acc_ref[:, s * LANES:(s + 1) * LANES]


def _moe_experts(xs, block_expert, n_valid, wg, wu, wd):
    n_rows = xs.shape[0] // ROW_CHUNKS
    bm, tf = MOE_ROWS, MOE_FCHUNK
    nf = FFN_EXPERT // tf
    fidx = lambda b, f, nv: jnp.where(b < nv[0], f, nf - 1)
    rows = pl.BlockSpec((bm * ROW_CHUNKS, LANES), lambda b, f, be, nv: (b, 0))
    grid_spec = pltpu.PrefetchScalarGridSpec(
        num_scalar_prefetch=2,
        grid=(n_rows // bm, nf),
        in_specs=[rows,
                  pl.BlockSpec((1, D_MODEL, tf), lambda b, f, be, nv: (be[b], 0, fidx(b, f, nv))),
                  pl.BlockSpec((1, D_MODEL, tf), lambda b, f, be, nv: (be[b], 0, fidx(b, f, nv))),
                  pl.BlockSpec((1, tf, D_MODEL), lambda b, f, be, nv: (be[b], fidx(b, f, nv), 0))],
        out_specs=rows,
        scratch_shapes=[pltpu.VMEM((bm, D_MODEL), BF16), pltpu.VMEM((bm, D_MODEL), F32)],
    )
    return pl.pallas_call(
        _moe_kernel,
        grid_spec=grid_spec,
        out_shape=jax.ShapeDtypeStruct((n_rows * ROW_CHUNKS, LANES), F32),
        compiler_params=_params(("arbitrary", "arbitrary")),
        name="moe_experts",
    )(block_expert, n_valid, xs, wg, wu, wd)


def _combine_kernel(dcur_ref, dnext_ref, x_ref, gate_ref, y_ref, o_ref, buf_ref, sem):
    n = GATHER_ROWS
    i = pl.program_id(0)
    slot = i % 2

    def issue(dref, sl):
        def body(j, carry):
            _row_copy(y_ref, dref[0, 0, j], buf_ref.at[sl], j, sem.at[sl]).start()
            return carry
        lax.fori_loop(0, 2 * n, body, 0, unroll=8)

    @pl.when(i == 0)
    def _():
        issue(dcur_ref, 0)

    @pl.when(i + 1 < pl.num_programs(0))
    def _():
        issue(dnext_ref, 1 - slot)

    def drain(j, carry):
        _row_copy(y_ref, 0, buf_ref.at[slot], 0, sem.at[slot]).wait()
        return carry

    lax.fori_loop(0, 2 * n, drain, 0, unroll=8)

    g1 = gate_ref[:, 0:1]
    g2 = gate_ref[:, 1:2]
    for s in range(ROW_CHUNKS):
        cs = slice(s * LANES, (s + 1) * LANES)
        o_ref[:, cs] = (x_ref[:, cs] + g1 * buf_ref[slot, _chunk_rows(s, n), :]
                        + g2 * buf_ref[slot, _chunk_rows(n * ROW_CHUNKS + s, n), :])


def _combine(x3, gates, ys, dest_blocks):
    t = x3.shape[0]
    n = GATHER_ROWS
    nt = t // n
    idx = lambda f: pl.BlockSpec((1, 1, 2 * n), f, memory_space=pltpu.SMEM)
    return pl.pallas_call(
        _combine_kernel,
        grid=(nt,),
        in_specs=[idx(lambda i: (i, 0, 0)), idx(lambda i: (jnp.minimum(i + 1, nt - 1), 0, 0)),
                  pl.BlockSpec((n, D_MODEL), lambda i: (i, 0)),
                  pl.BlockSpec((n, LANES), lambda i: (i, 0)),
                  pl.BlockSpec(memory_space=pl.ANY)],
        out_specs=pl.BlockSpec((n, D_MODEL), lambda i: (i, 0)),
        out_shape=jax.ShapeDtypeStruct((t, D_MODEL), F32),
        scratch_shapes=[pltpu.VMEM((2, 2 * n * ROW_CHUNKS, LANES), F32),
                        pltpu.SemaphoreType.DMA((2,))],
        compiler_params=_params(("arbitrary",)),
        name="moe_combine",
    )(dest_blocks, dest_blocks, x3, gates, ys)


def _even_weights(w_in, q_norm, k_norm):
    a = 2 * GLA_QK + 2 * GLA_V
    pad = jnp.zeros((D_MODEL, LANES - GLA_RANK), w_in.dtype)
    w = jnp.concatenate([w_in[:, :a], w_in[:, a + GLA_RANK:], w_in[:, a:a + GLA_RANK], pad], axis=1)
    qn = (jnp.tile(q_norm, DIL_HEADS) * DIL_HD ** -0.5).reshape(1, QK_DIL)
    kn = jnp.tile(k_norm, DIL_HEADS).reshape(1, QK_DIL)
    grp = np.zeros((QK_DIL, LANES), np.float32)
    grp[np.arange(QK_DIL), np.arange(QK_DIL) // DIL_HD] = 1.0
    return (w.astype(BF16), qn.astype(F32), kn.astype(F32),
            jnp.asarray(grp, BF16), jnp.asarray(grp.T.copy(), BF16))


def kernel(x, mix_norm_even, w_in_even, w_gla_gate_up, b_gla_gate, gla_out_norm, q_norm, k_norm,
           w_out_even, ffn_norm_even, w_ffn_gate, w_ffn_up, w_ffn_down, mix_norm_odd, w_in_odd,
           ret_group_norm, w_out_odd, ffn_norm_odd, w_router, w_exp_gate, w_exp_up, w_exp_down):
    bsz, seq, d = x.shape
    t = bsz * seq
    x2d = x.reshape(t, d)
    row_vec = lambda v: v.reshape(1, -1).astype(F32)

    w_in, qn, kn, grp, grpt = _even_weights(w_in_even[0], q_norm[0], k_norm[0])
    qk_a, v_a, r_a, q_b, k_b, v_b, lr = _in_even(x2d, row_vec(mix_norm_even[0]), w_in, qn, kn, grp, grpt)
    wgu = jnp.zeros((LANES, GLA_QK), F32).at[:GLA_RANK].set(w_gla_gate_up[0]).astype(BF16)
    o_a = _gla(qk_a, v_a, r_a, lr, wgu, row_vec(b_gla_gate[0]), row_vec(gla_out_norm[0]), bsz, seq)
    passes = [_dilated_pass(q_b, k_b, v_b, bsz, seq, dil) for _, dil in DIL_PATTERNS]
    w_out = w_out_even[0].astype(BF16)
    x2d = _out_even(x2d, o_a, [p[0] for p in passes], [p[1] for p in passes],
                    w_out[:GLA_V], w_out[GLA_V:])
    x2d = _ffn(x2d, row_vec(ffn_norm_even[0]),
               w_ffn_gate[0].astype(BF16), w_ffn_up[0].astype(BF16), w_ffn_down[0].astype(BF16))

    hcat = _in_odd(x2d, row_vec(mix_norm_odd[0]), w_in_odd[0].astype(BF16))
    o_c = _retention(hcat, row_vec(ret_group_norm[0]), bsz, seq)
    wr = jnp.zeros((D_MODEL, LANES), F32).at[:, :N_EXPERTS].set(w_router[0])
    wr_hi = wr.astype(BF16)
    wr_lo = (wr - wr_hi.astype(F32)).astype(BF16)
    x3, xn3, meta, gates, counts = _out_odd(x2d, o_c, w_out_odd[0].astype(BF16), row_vec(ffn_norm_odd[0]),
                                            jnp.concatenate([wr_hi, wr_lo], axis=1))

    bm = MOE_ROWS
    n_blocks = t * 2 // bm + N_EXPERTS
    n_rows = n_blocks * bm
    cnt = counts[0, :N_EXPERTS].astype(jnp.int32)
    padded = (cnt + bm - 1) // bm * bm
    pends = jnp.cumsum(padded)
    pstarts = pends - padded
    dest1 = pstarts[meta[:, 0]] + meta[:, 2]
    dest2 = pstarts[meta[:, 1]] + meta[:, 3]

    def dest_blocks(n):
        return jnp.concatenate([dest1.reshape(t // n, n), dest2.reshape(t // n, n)],
                               axis=1).reshape(t // n, 1, 2 * n).astype(jnp.int32)

    block_expert = jnp.minimum(
        jnp.searchsorted(pends, jnp.arange(n_blocks, dtype=jnp.int32) * bm, side='right'),
        N_EXPERTS - 1).astype(jnp.int32)
    n_valid = (pends[-1:] // bm).astype(jnp.int32)

    xs = _dispatch(xn3, dest_blocks(DISPATCH_ROWS), n_rows)
    ys = _moe_experts(xs, block_expert, n_valid, w_exp_gate[0].astype(BF16),
                      w_exp_up[0].astype(BF16), w_exp_down[0].astype(BF16))
    out = _combine(x3, gates, ys, dest_blocks(GATHER_ROWS))
    return out.reshape(bsz, seq, d)
```

```python
import numpy as np
import jax
import jax.numpy as jnp
from jax import lax
from jax.experimental import pallas as pl
from jax.experimental.pallas import tpu as pltpu

BF16 = jnp.bfloat16
F32 = jnp.float32

D_MODEL = 1024
NORM_EPS = 1e-6
GN_EPS = 1e-5
NEG_INF = -1e30

GLA_HEADS, GLA_DK, GLA_DV, GLA_RANK, GLA_TAU, GLA_CHUNK = 4, 128, 256, 16, 16.0, 64
DIL_HEADS, DIL_HD, DIL_BLOCK = 8, 64, 128
DIL_PATTERNS = ((128, 1), (512, 4), (2048, 16))
RET_HEADS, RET_DK, RET_DV = 4, 256, 512
FFN_DENSE = 2816
N_EXPERTS, FFN_EXPERT = 8, 3584

LANES = 128
SUBLANES = 8
ROW_CHUNKS = D_MODEL // LANES
VMEM_LIMIT = 56 * 1024 * 1024

ROW_TILE = 512
GLA_STEP = 256
RET_CHUNK = 256
FFN_CHUNK = 256
MOE_ROWS = 512
MOE_FCHUNK = 1792
MOE_INNER = 256
GATHER_ROWS = 256
DISPATCH_ROWS = 1024

QK_DIL = DIL_HEADS * DIL_HD
GLA_QK = GLA_HEADS * GLA_DK
GLA_V = GLA_HEADS * GLA_DV
EVEN_W = 2 * GLA_QK + 2 * GLA_V + 3 * QK_DIL + LANES
ODD_W = 2 * RET_HEADS * RET_DK + 2 * RET_HEADS * RET_DV


def _dot(a, b):
    return jnp.dot(a, b, preferred_element_type=F32)


def _dot_nt(a, b):
    return lax.dot_general(a, b, (((1,), (1,)), ((), ())), preferred_element_type=F32)


def _dot_tn(a, b):
    return lax.dot_general(a, b, (((0,), (0,)), ((), ())), preferred_element_type=F32)


def _split(x):
    hi = x.astype(BF16)
    lo = (x - hi.astype(F32)).astype(BF16)
    return hi, lo


def _rms(x, gain, eps=NORM_EPS):
    ms = jnp.mean(x * x, axis=-1, keepdims=True)
    return x * lax.rsqrt(ms + eps) * gain


def _silu(x):
    return x / (1.0 + jnp.exp(-x))


def _log_sigmoid(z):
    return jnp.minimum(z, 0.0) - jnp.log1p(jnp.exp(-jnp.abs(z)))


def _params(sem):
    return pltpu.CompilerParams(dimension_semantics=sem, vmem_limit_bytes=VMEM_LIMIT)


def _resident(shape):
    nd = len(shape)
    return pl.BlockSpec(shape, lambda *_: (0,) * nd, pipeline_mode=pl.Buffered(1))


def _head_inv_rms(acc, grp_ref, grpt_ref):
    ss = _dot((acc * acc).astype(BF16), grp_ref[...])
    inv = lax.rsqrt(ss * (1.0 / DIL_HD) + NORM_EPS)
    ih, il = _split(inv)
    return _dot(ih, grpt_ref[...]) + _dot(il, grpt_ref[...])


def _store_by_residue(val, nat_ref, res_refs, sc_ref):
    nat_ref[...] = val.astype(BF16)
    nc = QK_DIL // LANES
    for c in range(nc):
        sc_ref[c] = val[:, c * LANES:(c + 1) * LANES]
    for (_, d), ref in zip(DIL_PATTERNS[1:], res_refs):
        n = ROW_TILE // d
        for r in range(d):
            for c in range(nc):
                ref[r, :, c * LANES:(c + 1) * LANES] = sc_ref[c, pl.ds(r, n, stride=d), :].astype(BF16)


def _in_even_kernel(x_ref, g_ref, w_ref, qn_ref, kn_ref, grp_ref, grpt_ref,
                    qk_ref, v_ref, r_ref, lr_ref, q1_ref, k1_ref, v1_ref,
                    q4_ref, k4_ref, v4_ref, q16_ref, k16_ref, v16_ref, sc_ref):
    xn = _rms(x_ref[...], g_ref[...]).astype(BF16)
    c = 0
    for ref, width in ((qk_ref, 2 * GLA_QK), (v_ref, GLA_V), (r_ref, GLA_V)):
        ref[...] = _dot(xn, w_ref[:, c:c + width]).astype(BF16)
        c += width
    qb = _dot(xn, w_ref[:, c:c + QK_DIL])
    qb = qb * _head_inv_rms(qb, grp_ref, grpt_ref) * qn_ref[...]
    _store_by_residue(qb, q1_ref, (q4_ref, q16_ref), sc_ref)
    c += QK_DIL
    kb = _dot(xn, w_ref[:, c:c + QK_DIL])
    kb = kb * _head_inv_rms(kb, grp_ref, grpt_ref) * kn_ref[...]
    _store_by_residue(kb, k1_ref, (k4_ref, k16_ref), sc_ref)
    c += QK_DIL
    _store_by_residue(_dot(xn, w_ref[:, c:c + QK_DIL]), v1_ref, (v4_ref, v16_ref), sc_ref)
    c += QK_DIL
    lr_ref[...] = _dot(xn, w_ref[:, c:c + LANES])


def _in_even(x2d, gain, w, qn, kn, grp, grpt):
    t = x2d.shape[0]
    tm = ROW_TILE
    row = lambda width: pl.BlockSpec((tm, width), lambda i: (i, 0))
    res_spec = lambda d: pl.BlockSpec((d, tm // d, QK_DIL), lambda i: (0, i, 0))
    res_shape = lambda d: jax.ShapeDtypeStruct((d, t // d, QK_DIL), BF16)
    dils = [d for _, d in DIL_PATTERNS[1:]]
    out_shape = (
        jax.ShapeDtypeStruct((t, 2 * GLA_QK), BF16),
        jax.ShapeDtypeStruct((t, GLA_V), BF16),
        jax.ShapeDtypeStruct((t, GLA_V), BF16),
        jax.ShapeDtypeStruct((t, LANES), F32),
    ) + (jax.ShapeDtypeStruct((t, QK_DIL), BF16),) * 3 + tuple(res_shape(d) for d in dils for _ in range(3))
    return pl.pallas_call(
        _in_even_kernel,
        grid=(t // tm,),
        in_specs=[row(D_MODEL), _resident((1, D_MODEL)), _resident((D_MODEL, EVEN_W)),
                  _resident((1, QK_DIL)), _resident((1, QK_DIL)),
                  _resident((QK_DIL, LANES)), _resident((LANES, QK_DIL))],
        out_specs=[row(2 * GLA_QK), row(GLA_V), row(GLA_V), row(LANES)] + [row(QK_DIL)] * 3
                  + [res_spec(d) for d in dils for _ in range(3)],
        out_shape=out_shape,
        scratch_shapes=[pltpu.VMEM((QK_DIL // LANES, tm, LANES), F32)],
        compiler_params=_params(("parallel",)),
        name="in_even",
    )(x2d, gain, w, qn, kn, grp, grpt)


def _gla_kernel(qk_ref, v_ref, r_ref, lr_ref, wgu_ref, bg_ref, on_ref, o_ref, st_ref):
    c = GLA_CHUNK

    @pl.when(pl.program_id(1) == 0)
    def _():
        st_ref[...] = jnp.zeros_like(st_ref)

    rows = lax.broadcasted_iota(jnp.int32, (c, c), 0)
    cols = lax.broadcasted_iota(jnp.int32, (c, c), 1)
    causal = rows >= cols
    tri = causal.astype(BF16)

    def chunk(ci, carry):
        r0 = pl.multiple_of(ci * c, c)
        rs = pl.ds(r0, c)
        z = _dot(lr_ref[rs, :].astype(BF16), wgu_ref[...]) + bg_ref[...]
        log_a = _log_sigmoid(z) * (1.0 / GLA_TAU)
        hi, lo = _split(log_a)
        cum = _dot(tri, hi) + _dot(tri, lo)
        total = cum[c - 1:c, :]
        q = qk_ref[rs, 0:GLA_QK].astype(F32) * GLA_DK ** -0.5
        k = qk_ref[rs, GLA_QK:2 * GLA_QK].astype(F32)
        q_dec = (q * jnp.exp(cum)).astype(BF16)
        k_inv = (k * jnp.exp(-cum)).astype(BF16)
        k_end = (k * jnp.exp(total - cum)).astype(BF16)
        e_tot = jnp.exp(total)
        for h in range(GLA_HEADS):
            ks = slice(h * GLA_DK, (h + 1) * GLA_DK)
            vs = slice(h * GLA_DV, (h + 1) * GLA_DV)
            v_h = v_ref[rs, vs]
            att = jnp.where(causal, _dot_nt(q_dec[:, ks], k_inv[:, ks]), 0.0).astype(BF16)
            st = st_ref[h]
            o = _dot(att, v_h) + _dot_nt(q_dec[:, ks], st.astype(BF16))
            st_ref[h] = st * e_tot[:, ks] + _dot_tn(v_h, k_end[:, ks])
            gate = _silu(r_ref[rs, vs].astype(F32))
            o_ref[rs, vs] = (_rms(o, on_ref[...]) * gate).astype(BF16)
        return carry

    lax.fori_loop(0, GLA_STEP // c, chunk, 0, unroll=True)


def _gla(qk, v, r, lr, wgu, bg, on, bsz, seq):
    t = qk.shape[0]
    ns = seq // GLA_STEP
    row = lambda width: pl.BlockSpec((GLA_STEP, width), lambda b, s: (b * ns + s, 0))
    return pl.pallas_call(
        _gla_kernel,
        grid=(bsz, ns),
        in_specs=[row(2 * GLA_QK), row(GLA_V), row(GLA_V), row(LANES),
                  _resident((LANES, GLA_QK)), _resident((1, GLA_QK)), _resident((1, GLA_DV))],
        out_specs=row(GLA_V),
        out_shape=jax.ShapeDtypeStruct((t, GLA_V), BF16),
        scratch_shapes=[pltpu.VMEM((GLA_HEADS, GLA_DV, GLA_DK), F32)],
        compiler_params=_params(("parallel", "arbitrary")),
        name="gla",
    )(qk, v, r, lr, wgu, bg, on)


def _dil_kernel(q_ref, k_ref, v_ref, e_ref, b_ref, a_ref, o_ref, lse_ref, kp_ref, vp_ref):
    blk = DIL_BLOCK
    n = pl.program_id(2)

    @pl.when(n == 0)
    def _():
        kp_ref[...] = jnp.zeros_like(kp_ref)
        vp_ref[...] = jnp.zeros_like(vp_ref)

    qi = lax.broadcasted_iota(jnp.int32, (2 * blk, 2 * blk), 0) & (blk - 1)
    kj = lax.broadcasted_iota(jnp.int32, (2 * blk, 2 * blk), 1)
    valid = ((kj < blk) & (kj >= qi) & (n > 0)) | ((kj >= blk) & (kj - blk <= qi))
    first = lax.broadcasted_iota(jnp.int32, (blk, LANES), 1) < DIL_HD
    ones = jnp.ones((2 * blk, LANES), BF16)
    for c in range(QK_DIL // LANES):
        cs = slice(c * LANES, (c + 1) * LANES)
        q = q_ref[0, :, cs]
        zero = jnp.zeros_like(q)
        q2 = jnp.concatenate([jnp.where(first, q, zero), jnp.where(first, zero, q)], axis=0)
        q_aug = jnp.concatenate([q2, e_ref[...]], axis=1)
        k_aug = jnp.concatenate([jnp.concatenate([kp_ref[:, cs], k_ref[0, :, cs]], axis=0), b_ref[c]], axis=1)
        v_aug = jnp.concatenate([jnp.concatenate([vp_ref[:, cs], v_ref[0, :, cs]], axis=0), ones], axis=1)
        s = jnp.where(valid, _dot_nt(q_aug, k_aug), NEG_INF)
        m = jnp.max(s, axis=-1, keepdims=True)
        p = jnp.exp(s - m).astype(BF16)
        ov = _dot(p, v_aug)
        den = ov[:, LANES:]
        o2 = ov[:, :LANES] / den
        l2 = m + jnp.log(den) + a_ref[c]
        o_ref[0, :, cs] = jnp.where(first, o2[:blk], o2[blk:]).astype(BF16)
        lse_ref[0, :, cs] = jnp.where(first, l2[:blk], l2[blk:])
    kp_ref[...] = k_ref[0]
    vp_ref[...] = v_ref[0]


def _dilated_tables(dilation):
    blk = DIL_BLOCK
    slopes = np.exp2(-8.0 * np.arange(1, DIL_HEADS + 1) / DIL_HEADS)
    npair = QK_DIL // LANES
    e = np.zeros((2 * blk, LANES), np.float32)
    e[:blk, 0] = 1.0
    e[blk:, 1] = 1.0
    b = np.zeros((npair, 2 * blk, LANES), np.float32)
    a = np.zeros((npair, 2 * blk, LANES), np.float32)
    key_off = np.arange(2 * blk, dtype=np.float64) - blk
    qpos = np.arange(blk, dtype=np.float64)
    for c in range(npair):
        for half in range(2):
            sl = slopes[2 * c + half] * dilation
            b[c, :, half] = sl * key_off
            a[c, half * blk:(half + 1) * blk, :] = (-sl * qpos)[:, None]
    return jnp.asarray(e, BF16), jnp.asarray(b, BF16), jnp.asarray(a, F32)


def _dilated_pass(q, k, v, bsz, seq, dilation):
    seg = seq // dilation
    nb = seg // DIL_BLOCK
    e, b, a = _dilated_tables(dilation)
    blk = pl.BlockSpec((1, DIL_BLOCK, QK_DIL), lambda bi, r, n: (r, bi * nb + n, 0))
    shape = (dilation, bsz * seg, QK_DIL)
    return pl.pallas_call(
        _dil_kernel,
        grid=(bsz, dilation, nb),
        in_specs=[blk, blk, blk, _resident(e.shape), _resident(b.shape), _resident(a.shape)],
        out_specs=[blk, blk],
        out_shape=(jax.ShapeDtypeStruct(shape, BF16), jax.ShapeDtypeStruct(shape, F32)),
        scratch_shapes=[pltpu.VMEM((DIL_BLOCK, QK_DIL), BF16), pltpu.VMEM((DIL_BLOCK, QK_DIL), BF16)],
        compiler_params=_params(("parallel", "parallel", "arbitrary")),
        name=f"dilated_d{dilation}",
    )(q, k, v, e, b, a)


def _to_natural(ref, d, sc_ref):
    n = ROW_TILE // d
    for r in range(d):
        for c in range(QK_DIL // LANES):
            sc_ref[c, pl.ds(r, n, stride=d), :] = ref[r, :, c * LANES:(c + 1) * LANES].astype(F32)


def _out_even_kernel(x_ref, oa_ref, o1_ref, l1_ref, o4_ref, l4_ref, o16_ref, l16_ref,
                     wa_ref, wb_ref, y_ref, so4_ref, sl4_ref, so16_ref, sl16_ref, ob_ref):
    for (_, d), o_ref, l_ref, so_ref, sl_ref in zip(DIL_PATTERNS[1:], (o4_ref, o16_ref), (l4_ref, l16_ref),
                                                     (so4_ref, so16_ref), (sl4_ref, sl16_ref)):
        _to_natural(o_ref, d, so_ref)
        _to_natural(l_ref, d, sl_ref)
    for c in range(QK_DIL // LANES):
        cs = slice(c * LANES, (c + 1) * LANES)
        l1, l2, l3 = l1_ref[:, cs], sl4_ref[c], sl16_ref[c]
        m = jnp.maximum(jnp.maximum(l1, l2), l3)
        e1, e2, e3 = jnp.exp(l1 - m), jnp.exp(l2 - m), jnp.exp(l3 - m)
        ob = (e1 * o1_ref[:, cs].astype(F32) + e2 * so4_ref[c] + e3 * so16_ref[c]) / (e1 + e2 + e3)
        ob_ref[:, cs] = ob.astype(BF16)
    y_ref[...] = x_ref[...] + _dot(oa_ref[...], wa_ref[...]) + _dot(ob_ref[...], wb_ref[...])


def _out_even(x2d, oa, passes, wa, wb):
    t = x2d.shape[0]
    tm = ROW_TILE
    row = lambda width: pl.BlockSpec((tm, width), lambda i: (i, 0))
    res = lambda d: pl.BlockSpec((d, tm // d, QK_DIL), lambda i: (0, i, 0))
    (o1, l1), (o4, l4), (o16, l16) = passes
    d4, d16 = DIL_PATTERNS[1][1], DIL_PATTERNS[2][1]
    sc = pltpu.VMEM((QK_DIL // LANES, tm, LANES), F32)
    return pl.pallas_call(
        _out_even_kernel,
        grid=(t // tm,),
        in_specs=[row(D_MODEL), row(GLA_V), row(QK_DIL), row(QK_DIL), res(d4), res(d4), res(d16), res(d16),
                  _resident((GLA_V, D_MODEL)), _resident((QK_DIL, D_MODEL))],
        out_specs=row(D_MODEL),
        out_shape=jax.ShapeDtypeStruct((t, D_MODEL), F32),
        scratch_shapes=[sc, sc, sc, sc, pltpu.VMEM((tm, QK_DIL), BF16)],
        compiler_params=_params(("parallel",)),
        name="out_even",
    )(x2d, oa, o1.reshape(t, QK_DIL), l1.reshape(t, QK_DIL), o4, l4, o16, l16, wa, wb)


def _ffn_kernel(x_ref, g_ref, wg_ref, wu_ref, wd_ref, y_ref, xn_ref, h_ref):
    x = x_ref[...]
    xn_ref[...] = _rms(x, g_ref[...]).astype(BF16)
    for ci in range(FFN_DENSE // FFN_CHUNK):
        cs = slice(ci * FFN_CHUNK, (ci + 1) * FFN_CHUNK)
        xn = xn_ref[...]
        h_ref[:, cs] = (_silu(_dot(xn, wg_ref[:, cs])) * _dot(xn, wu_ref[:, cs])).astype(BF16)
    y_ref[...] = x + _dot(h_ref[...], wd_ref[...])


def _ffn(x2d, gain, wg, wu, wd):
    t = x2d.shape[0]
    tm = ROW_TILE
    row = pl.BlockSpec((tm, D_MODEL), lambda i: (i, 0))
    return pl.pallas_call(
        _ffn_kernel,
        grid=(t // tm,),
        in_specs=[row, _resident((1, D_MODEL)), _resident((D_MODEL, FFN_DENSE)),
                  _resident((D_MODEL, FFN_DENSE)), _resident((FFN_DENSE, D_MODEL))],
        out_specs=row,
        out_shape=jax.ShapeDtypeStruct((t, D_MODEL), F32),
        scratch_shapes=[pltpu.VMEM((tm, D_MODEL), BF16), pltpu.VMEM((tm, FFN_DENSE), BF16)],
        compiler_params=_params(("parallel",)),
        name="ffn_dense",
    )(x2d, gain, wg, wu, wd)


def _in_odd_kernel(x_ref, g_ref, w_ref, h_ref):
    xn = _rms(x_ref[...], g_ref[...]).astype(BF16)
    width = RET_HEADS * RET_DK
    for ci in range(ODD_W // width):
        cs = slice(ci * width, (ci + 1) * width)
        acc = _dot(xn, w_ref[:, cs])
        if ci == 1:
            acc = acc * RET_DK ** -0.5
        h_ref[:, cs] = acc.astype(BF16)


def _in_odd(x2d, gain, w):
    t = x2d.shape[0]
    tm = ROW_TILE
    return pl.pallas_call(
        _in_odd_kernel,
        grid=(t // tm,),
        in_specs=[pl.BlockSpec((tm, D_MODEL), lambda i: (i, 0)), _resident((1, D_MODEL)),
                  _resident((D_MODEL, ODD_W))],
        out_specs=pl.BlockSpec((tm, ODD_W), lambda i: (i, 0)),
        out_shape=jax.ShapeDtypeStruct((t, ODD_W), BF16),
        compiler_params=_params(("parallel",)),
        name="in_odd",
    )(x2d, gain, w)


def _ret_kernel(h_ref, dec_ref, qd_ref, kd_ref, cd_ref, gn_ref, o_ref, st_ref):
    @pl.when(pl.program_id(1) == 0)
    def _():
        st_ref[...] = jnp.zeros_like(st_ref)

    nk = RET_HEADS * RET_DK
    for h in range(RET_HEADS):
        ks = slice(h * RET_DK, (h + 1) * RET_DK)
        vs = slice(h * RET_DV, (h + 1) * RET_DV)
        q = h_ref[:, ks]
        k = h_ref[:, nk + h * RET_DK:nk + (h + 1) * RET_DK]
        v = h_ref[:, 2 * nk + h * RET_DV:2 * nk + (h + 1) * RET_DV]
        g = h_ref[:, 2 * nk + (RET_HEADS + h) * RET_DV:2 * nk + (RET_HEADS + h + 1) * RET_DV]
        att = (_dot_nt(q, k) * dec_ref[h]).astype(BF16)
        q_dec = (q.astype(F32) * qd_ref[h]).astype(BF16)
        k_dec = (k.astype(F32) * kd_ref[h]).astype(BF16)
        st = st_ref[h]
        o = _dot(att, v) + _dot(q_dec, st.astype(BF16))
        st_ref[h] = st * cd_ref[h] + _dot_tn(k_dec, v)
        mu = jnp.mean(o, axis=-1, keepdims=True)
        oc = o - mu
        var = jnp.mean(oc * oc, axis=-1, keepdims=True)
        gate = _silu(g.astype(F32))
        o_ref[:, vs] = (oc * lax.rsqrt(var + GN_EPS) * gn_ref[:, vs] * gate).astype(BF16)


def _retention_tables():
    h, c = RET_HEADS, RET_CHUNK
    log_gamma = np.log1p(-np.exp2(-5.0 - np.arange(h, dtype=np.float64)))
    pos = np.arange(c, dtype=np.float64)
    rel = pos[:, None] - pos[None, :]
    decay = np.where(rel >= 0, np.exp(log_gamma[:, None, None] * np.maximum(rel, 0.0)), 0.0)
    q_dec = np.exp(log_gamma[:, None] * (pos + 1.0))[:, :, None]
    k_dec = np.exp(log_gamma[:, None] * (c - 1.0 - pos))[:, :, None]
    chunk_decay = np.broadcast_to(np.exp(log_gamma * c)[:, None, None], (h, 1, RET_DV))
    f = lambda a: jnp.asarray(np.ascontiguousarray(a), F32)
    return f(decay), f(q_dec), f(k_dec), f(chunk_decay)


def _retention(hcat, gn_w, bsz, seq):
    t = hcat.shape[0]
    c = RET_CHUNK
    ns = seq // c
    decay, q_dec, k_dec, chunk_decay = _retention_tables()
    return pl.pallas_call(
        _ret_kernel,
        grid=(bsz, ns),
        in_specs=[pl.BlockSpec((c, ODD_W), lambda b, s: (b * ns + s, 0)),
                  _resident((RET_HEADS, c, c)), _resident((RET_HEADS, c, 1)),
                  _resident((RET_HEADS, c, 1)), _resident((RET_HEADS, 1, RET_DV)),
                  _resident((1, RET_HEADS * RET_DV))],
        out_specs=pl.BlockSpec((c, RET_HEADS * RET_DV), lambda b, s: (b * ns + s, 0)),
        out_shape=jax.ShapeDtypeStruct((t, RET_HEADS * RET_DV), BF16),
        scratch_shapes=[pltpu.VMEM((RET_HEADS, RET_DK, RET_DV), F32)],
        compiler_params=_params(("parallel", "arbitrary")),
        name="retention",
    )(hcat, decay, q_dec, k_dec, chunk_decay, gn_w)


def _out_odd_kernel(x_ref, oc_ref, w_ref, g_ref, wr_ref, tri_ref,
                    y_ref, xn_ref, meta_ref, gate_ref, cnt_ref, run_ref):
    @pl.when(pl.program_id(0) == 0)
    def _():
        run_ref[...] = jnp.zeros_like(run_ref)

    y = x_ref[...] + _dot(oc_ref[...], w_ref[...])
    y_ref[...] = y
    xn = _rms(y, g_ref[...])
    for s in range(ROW_CHUNKS):
        xn_ref[_chunk_rows(s, ROW_TILE), :] = xn[:, s * LANES:(s + 1) * LANES]

    hi, lo = _split(xn)
    both = _dot(hi, wr_ref[...])
    logits = both[:, :LANES] + both[:, LANES:] + _dot(lo, wr_ref[:, :LANES])
    lane = lax.broadcasted_iota(jnp.int32, logits.shape, 1).astype(F32)
    logits = jnp.where(lane < N_EXPERTS, logits, -jnp.inf)
    t1 = jnp.max(logits, axis=-1, keepdims=True)
    i1 = jnp.min(jnp.where(logits == t1, lane, float(LANES)), axis=-1, keepdims=True)
    rest = jnp.where(lane == i1, -jnp.inf, logits)
    t2 = jnp.max(rest, axis=-1, keepdims=True)
    i2 = jnp.min(jnp.where(rest == t2, lane, float(LANES)), axis=-1, keepdims=True)
    e2 = jnp.exp(t2 - t1)
    den = 1.0 + e2
    gate_ref[...] = jnp.where(lane == 0.0, 1.0 / den, jnp.where(lane == 1.0, e2 / den, 0.0))

    sel = (lane == i1) | (lane == i2)
    rank = _dot(tri_ref[...], sel.astype(BF16)) + run_ref[...]
    r1 = jnp.sum(jnp.where(lane == i1, rank, 0.0), axis=-1, keepdims=True)
    r2 = jnp.sum(jnp.where(lane == i2, rank, 0.0), axis=-1, keepdims=True)
    run_ref[...] += jnp.sum(sel.astype(F32), axis=0, keepdims=True)
    cnt_ref[...] = run_ref[...]
    meta = jnp.where(lane == 0.0, i1, jnp.where(lane == 1.0, i2,
                     jnp.where(lane == 2.0, r1, jnp.where(lane == 3.0, r2, 0.0))))
    meta_ref[...] = meta.astype(jnp.int32)


def _out_odd(x2d, oc, w, gain, wr):
    t = x2d.shape[0]
    tm = ROW_TILE
    tri = jnp.asarray(np.tril(np.ones((tm, tm), np.float32), -1), BF16)
    row = lambda width: pl.BlockSpec((tm, width), lambda i: (i, 0))
    return pl.pallas_call(
        _out_odd_kernel,
        grid=(t // tm,),
        in_specs=[row(D_MODEL), row(RET_HEADS * RET_DV), _resident((RET_HEADS * RET_DV, D_MODEL)),
                  _resident((1, D_MODEL)), _resident((D_MODEL, 2 * LANES)), _resident((tm, tm))],
        out_specs=[row(D_MODEL), pl.BlockSpec((tm * ROW_CHUNKS, LANES), lambda i: (i, 0)),
                   row(LANES), row(LANES), pl.BlockSpec((1, LANES), lambda i: (0, 0))],
        out_shape=(jax.ShapeDtypeStruct((t, D_MODEL), F32),
                   jax.ShapeDtypeStruct((t * ROW_CHUNKS, LANES), F32),
                   jax.ShapeDtypeStruct((t, LANES), jnp.int32),
                   jax.ShapeDtypeStruct((t, LANES), F32),
                   jax.ShapeDtypeStruct((1, LANES), F32)),
        scratch_shapes=[pltpu.VMEM((1, LANES), F32)],
        compiler_params=_params(("arbitrary",)),
        name="out_odd_router",
    )(x2d, oc, w, gain, wr, tri)


def _chunk_rows(s, n_rows):
    return pl.ds(s, n_rows, stride=ROW_CHUNKS)


def _tile_row(ref, row):
    return ref.at[pl.ds(pl.multiple_of(row * ROW_CHUNKS, ROW_CHUNKS), ROW_CHUNKS), :]


def _row_copy(src_ref, src_row, dst_ref, dst_row, sem):
    return pltpu.make_async_copy(_tile_row(src_ref, src_row), _tile_row(dst_ref, dst_row), sem)


def _dispatch_kernel(dest_ref, x_ref, init_ref, xs_ref, sem):
    del init_ref
    n = DISPATCH_ROWS

    def issue(j, carry):
        _row_copy(x_ref, j, xs_ref, dest_ref[0, 0, j], sem).start()
        _row_copy(x_ref, j, xs_ref, dest_ref[0, 0, n + j], sem).start()
        return carry

    lax.fori_loop(0, n, issue, 0, unroll=8)

    def drain(j, carry):
        _row_copy(x_ref, 0, xs_ref, 0, sem).wait()
        return carry

    lax.fori_loop(0, 2 * n, drain, 0, unroll=8)


def _dispatch(xn3, dest_blocks, n_rows):
    t = xn3.shape[0] // ROW_CHUNKS
    n = DISPATCH_ROWS
    zeros = jnp.zeros((n_rows * ROW_CHUNKS, LANES), F32)
    return pl.pallas_call(
        _dispatch_kernel,
        grid=(t // n,),
        in_specs=[pl.BlockSpec((1, 1, 2 * n), lambda i: (i, 0, 0), memory_space=pltpu.SMEM),
                  pl.BlockSpec((n * ROW_CHUNKS, LANES), lambda i: (i, 0)),
                  pl.BlockSpec(memory_space=pl.ANY)],
        out_specs=pl.BlockSpec(memory_space=pl.ANY),
        out_shape=jax.ShapeDtypeStruct((n_rows * ROW_CHUNKS, LANES), F32),
        scratch_shapes=[pltpu.SemaphoreType.DMA(())],
        input_output_aliases={2: 0},
        compiler_params=_params(("arbitrary",)),
        name="moe_dispatch",
    )(dest_blocks, xn3, zeros)


def _moe_kernel(be_ref, nv_ref, x_ref, wg_ref, wu_ref, wd_ref, y_ref, xb_ref, h_ref, acc_ref):
    del be_ref
    b = pl.program_id(0)
    f = pl.program_id(1)
    nf = pl.num_programs(1)

    @pl.when(f == 0)
    def _():
        acc_ref[...] = jnp.zeros_like(acc_ref)
        for s in range(ROW_CHUNKS):
            xb_ref[:, s * LANES:(s + 1) * LANES] = x_ref[_chunk_rows(s, MOE_ROWS), :].astype(BF16)

    @pl.when(b < nv_ref[0])
    def _():
        for ci in range(MOE_FCHUNK // MOE_INNER):
            cs = slice(ci * MOE_INNER, (ci + 1) * MOE_INNER)
            x = xb_ref[...]
            h_ref[:, cs] = (_silu(_dot(x, wg_ref[0, :, cs])) * _dot(x, wu_ref[0, :, cs])).astype(BF16)
        acc_ref[...] += _dot(h_ref[...], wd_ref[0])

    @pl.when(f == nf - 1)
    def _():
        for s in range(ROW_CHUNKS):
            y_ref[_chunk_rows(s, MOE_ROWS), :] = acc_ref[:, s * LANES:(s + 1) * LANES]


def _moe_experts(xs, block_expert, n_valid, wg, wu, wd):
    n_rows = xs.shape[0] // ROW_CHUNKS
    bm, tf = MOE_ROWS, MOE_FCHUNK
    nf = FFN_EXPERT // tf
    fidx = lambda b, f, nv: jnp.where(b < nv[0], f, nf - 1)
    rows = pl.BlockSpec((bm * ROW_CHUNKS, LANES), lambda b, f, be, nv: (b, 0))
    grid_spec = pltpu.PrefetchScalarGridSpec(
        num_scalar_prefetch=2,
        grid=(n_rows // bm, nf),
        in_specs=[rows,
                  pl.BlockSpec((1, D_MODEL, tf), lambda b, f, be, nv: (be[b], 0, fidx(b, f, nv))),
                  pl.BlockSpec((1, D_MODEL, tf), lambda b, f, be, nv: (be[b], 0, fidx(b, f, nv))),
                  pl.BlockSpec((1, tf, D_MODEL), lambda b, f, be, nv: (be[b], fidx(b, f, nv), 0))],
        out_specs=rows,
        scratch_shapes=[pltpu.VMEM((bm, D_MODEL), BF16), pltpu.VMEM((bm, tf), BF16),
                        pltpu.VMEM((bm, D_MODEL), F32)],
    )
    return pl.pallas_call(
        _moe_kernel,
        grid_spec=grid_spec,
        out_shape=jax.ShapeDtypeStruct((n_rows * ROW_CHUNKS, LANES), F32),
        compiler_params=_params(("arbitrary", "arbitrary")),
        name="moe_experts",
    )(block_expert, n_valid, xs, wg, wu, wd)


def _combine_kernel(dcur_ref, dnext_ref, x_ref, gate_ref, y_ref, o_ref, buf_ref, sem):
    n = GATHER_ROWS
    i = pl.program_id(0)
    slot = i % 2

    def issue(dref, sl):
        def body(j, carry):
            _row_copy(y_ref, dref[0, 0, j], buf_ref.at[sl], j, sem.at[sl]).start()
            return carry
        lax.fori_loop(0, 2 * n, body, 0, unroll=8)

    @pl.when(i == 0)
    def _():
        issue(dcur_ref, 0)

    @pl.when(i + 1 < pl.num_programs(0))
    def _():
        issue(dnext_ref, 1 - slot)

    def drain(j, carry):
        _row_copy(y_ref, 0, buf_ref.at[slot], 0, sem.at[slot]).wait()
        return carry

    lax.fori_loop(0, 2 * n, drain, 0, unroll=8)

    g1 = gate_ref[:, 0:1]
    g2 = gate_ref[:, 1:2]
    for s in range(ROW_CHUNKS):
        cs = slice(s * LANES, (s + 1) * LANES)
        o_ref[:, cs] = (x_ref[:, cs] + g1 * buf_ref[slot, _chunk_rows(s, n), :]
                        + g2 * buf_ref[slot, _chunk_rows(n * ROW_CHUNKS + s, n), :])


def _combine(x3, gates, ys, dest_blocks):
    t = x3.shape[0]
    n = GATHER_ROWS
    nt = t // n
    idx = lambda f: pl.BlockSpec((1, 1, 2 * n), f, memory_space=pltpu.SMEM)
    return pl.pallas_call(
        _combine_kernel,
        grid=(nt,),
        in_specs=[idx(lambda i: (i, 0, 0)), idx(lambda i: (jnp.minimum(i + 1, nt - 1), 0, 0)),
                  pl.BlockSpec((n, D_MODEL), lambda i: (i, 0)),
                  pl.BlockSpec((n, LANES), lambda i: (i, 0)),
                  pl.BlockSpec(memory_space=pl.ANY)],
        out_specs=pl.BlockSpec((n, D_MODEL), lambda i: (i, 0)),
        out_shape=jax.ShapeDtypeStruct((t, D_MODEL), F32),
        scratch_shapes=[pltpu.VMEM((2, 2 * n * ROW_CHUNKS, LANES), F32),
                        pltpu.SemaphoreType.DMA((2,))],
        compiler_params=_params(("arbitrary",)),
        name="moe_combine",
    )(dest_blocks, dest_blocks, x3, gates, ys)


def _even_weights(w_in, q_norm, k_norm):
    a = 2 * GLA_QK + 2 * GLA_V
    pad = jnp.zeros((D_MODEL, LANES - GLA_RANK), w_in.dtype)
    w = jnp.concatenate([w_in[:, :a], w_in[:, a + GLA_RANK:], w_in[:, a:a + GLA_RANK], pad], axis=1)
    qn = (jnp.tile(q_norm, DIL_HEADS) * DIL_HD ** -0.5).reshape(1, QK_DIL)
    kn = jnp.tile(k_norm, DIL_HEADS).reshape(1, QK_DIL)
    grp = np.zeros((QK_DIL, LANES), np.float32)
    grp[np.arange(QK_DIL), np.arange(QK_DIL) // DIL_HD] = 1.0
    return (w.astype(BF16), qn.astype(F32), kn.astype(F32),
            jnp.asarray(grp, BF16), jnp.asarray(grp.T.copy(), BF16))


def kernel(x, mix_norm_even, w_in_even, w_gla_gate_up, b_gla_gate, gla_out_norm, q_norm, k_norm,
           w_out_even, ffn_norm_even, w_ffn_gate, w_ffn_up, w_ffn_down, mix_norm_odd, w_in_odd,
           ret_group_norm, w_out_odd, ffn_norm_odd, w_router, w_exp_gate, w_exp_up, w_exp_down):
    bsz, seq, d = x.shape
    t = bsz * seq
    x2d = x.reshape(t, d)
    row_vec = lambda v: v.reshape(1, -1).astype(F32)

    w_in, qn, kn, grp, grpt = _even_weights(w_in_even[0], q_norm[0], k_norm[0])
    (qk_a, v_a, r_a, lr, q1, k1, v1, q4, k4, v4, q16, k16, v16) = _in_even(
        x2d, row_vec(mix_norm_even[0]), w_in, qn, kn, grp, grpt)
    wgu = jnp.zeros((LANES, GLA_QK), F32).at[:GLA_RANK].set(w_gla_gate_up[0]).astype(BF16)
    o_a = _gla(qk_a, v_a, r_a, lr, wgu, row_vec(b_gla_gate[0]), row_vec(gla_out_norm[0]), bsz, seq)
    nat = lambda a: a.reshape(1, t, QK_DIL)
    passes = [_dilated_pass(q, k, v, bsz, seq, dil) for (_, dil), (q, k, v) in
              zip(DIL_PATTERNS, ((nat(q1), nat(k1), nat(v1)), (q4, k4, v4), (q16, k16, v16)))]
    w_out = w_out_even[0].astype(BF16)
    x2d = _out_even(x2d, o_a, passes, w_out[:GLA_V], w_out[GLA_V:])
    x2d = _ffn(x2d, row_vec(ffn_norm_even[0]),
               w_ffn_gate[0].astype(BF16), w_ffn_up[0].astype(BF16), w_ffn_down[0].astype(BF16))

    hcat = _in_odd(x2d, row_vec(mix_norm_odd[0]), w_in_odd[0].astype(BF16))
    o_c = _retention(hcat, row_vec(ret_group_norm[0]), bsz, seq)
    wr = jnp.zeros((D_MODEL, LANES), F32).at[:, :N_EXPERTS].set(w_router[0])
    wr_hi = wr.astype(BF16)
    wr_lo = (wr - wr_hi.astype(F32)).astype(BF16)
    x3, xn3, meta, gates, counts = _out_odd(x2d, o_c, w_out_odd[0].astype(BF16), row_vec(ffn_norm_odd[0]),
                                            jnp.concatenate([wr_hi, wr_lo], axis=1))

    bm = MOE_ROWS
    n_blocks = t * 2 // bm + N_EXPERTS
    n_rows = n_blocks * bm
    cnt = counts[0, :N_EXPERTS].astype(jnp.int32)
    padded = (cnt + bm - 1) // bm * bm
    pends = jnp.cumsum(padded)
    pstarts = pends - padded
    dest1 = pstarts[meta[:, 0]] + meta[:, 2]
    dest2 = pstarts[meta[:, 1]] + meta[:, 3]

    def dest_blocks(n):
        return jnp.concatenate([dest1.reshape(t // n, n), dest2.reshape(t // n, n)],
                               axis=1).reshape(t // n, 1, 2 * n).astype(jnp.int32)

    block_expert = jnp.minimum(
        jnp.searchsorted(pends, jnp.arange(n_blocks, dtype=jnp.int32) * bm, side='right'),
        N_EXPERTS - 1).astype(jnp.int32)
    n_valid = (pends[-1:] // bm).astype(jnp.int32)

    xs = _dispatch(xn3, dest_blocks(DISPATCH_ROWS), n_rows)
    ys = _moe_experts(xs, block_expert, n_valid, w_exp_gate[0].astype(BF16),
                      w_exp_up[0].astype(BF16), w_exp_down[0].astype(BF16))
    out = _combine(x3, gates, ys, dest_blocks(GATHER_ROWS))
    return out.reshape(bsz, seq, d)
```

```python
import numpy as np
import jax
import jax.numpy as jnp
from jax import lax
from jax.experimental import pallas as pl
from jax.experimental.pallas import tpu as pltpu

BF16 = jnp.bfloat16
F32 = jnp.float32

D_MODEL = 1024
NORM_EPS = 1e-6
GN_EPS = 1e-5
NEG_INF = -1e30

GLA_HEADS, GLA_DK, GLA_DV, GLA_RANK, GLA_TAU, GLA_CHUNK = 4, 128, 256, 16, 16.0, 64
DIL_HEADS, DIL_HD, DIL_BLOCK = 8, 64, 128
DIL_PATTERNS = ((128, 1), (512, 4), (2048, 16))
RET_HEADS, RET_DK, RET_DV = 4, 256, 512
FFN_DENSE = 2816
N_EXPERTS, FFN_EXPERT = 8, 3584

LANES = 128
SUBLANES = 8
ROW_CHUNKS = D_MODEL // LANES
VMEM_LIMIT = 56 * 1024 * 1024

ROW_TILE = 512
GLA_STEP = 256
RET_CHUNK = 256
FFN_CHUNK = 256
DIL_GROUPS = 4
MOE_ROWS = 512
MOE_FCHUNK = 1792
MOE_INNER = 256
GATHER_ROWS = 256
DISPATCH_ROWS = 1024

QK_DIL = DIL_HEADS * DIL_HD
GLA_QK = GLA_HEADS * GLA_DK
GLA_V = GLA_HEADS * GLA_DV
EVEN_W = 2 * GLA_QK + 2 * GLA_V + 3 * QK_DIL + LANES
ODD_W = 2 * RET_HEADS * RET_DK + 2 * RET_HEADS * RET_DV


def _dot(a, b):
    return jnp.dot(a, b, preferred_element_type=F32)


def _dot_nt(a, b):
    return lax.dot_general(a, b, (((1,), (1,)), ((), ())), preferred_element_type=F32)


def _dot_tn(a, b):
    return lax.dot_general(a, b, (((0,), (0,)), ((), ())), preferred_element_type=F32)


def _split(x):
    hi = x.astype(BF16)
    lo = (x - hi.astype(F32)).astype(BF16)
    return hi, lo


def _rms(x, gain, eps=NORM_EPS):
    ms = jnp.mean(x * x, axis=-1, keepdims=True)
    return x * lax.rsqrt(ms + eps) * gain


def _silu(x):
    return x / (1.0 + jnp.exp(-x))


def _log_sigmoid(z):
    return jnp.minimum(z, 0.0) - jnp.log1p(jnp.exp(-jnp.abs(z)))


def _params(sem):
    return pltpu.CompilerParams(dimension_semantics=sem, vmem_limit_bytes=VMEM_LIMIT)


def _resident(shape):
    nd = len(shape)
    return pl.BlockSpec(shape, lambda *_: (0,) * nd, pipeline_mode=pl.Buffered(1))


def _head_inv_rms(acc, grp_ref, grpt_ref):
    ss = _dot((acc * acc).astype(BF16), grp_ref[...])
    inv = lax.rsqrt(ss * (1.0 / DIL_HD) + NORM_EPS)
    ih, il = _split(inv)
    return _dot(ih, grpt_ref[...]) + _dot(il, grpt_ref[...])


DIL_STRIDE = 4
assert [d for _, d in DIL_PATTERNS] == [1, DIL_STRIDE, DIL_STRIDE * DIL_STRIDE]


def _store_by_residue(val, nat_ref, r4_ref, r16_ref, sc_ref, sc4_ref):
    nat_ref[...] = val.astype(BF16)
    nc = QK_DIL // LANES
    n4 = ROW_TILE // DIL_STRIDE
    n16 = n4 // DIL_STRIDE
    for c in range(nc):
        sc_ref[c] = val[:, c * LANES:(c + 1) * LANES]
    for b in range(DIL_STRIDE):
        for c in range(nc):
            y = sc_ref[c, pl.ds(b, n4, stride=DIL_STRIDE), :]
            r4_ref[b, :, c * LANES:(c + 1) * LANES] = y.astype(BF16)
            sc4_ref[b, c] = y
    for b in range(DIL_STRIDE):
        for a in range(DIL_STRIDE):
            for c in range(nc):
                r16_ref[DIL_STRIDE * a + b, :, c * LANES:(c + 1) * LANES] = (
                    sc4_ref[b, c, pl.ds(a, n16, stride=DIL_STRIDE), :].astype(BF16))


def _in_even_kernel(x_ref, g_ref, w_ref, qn_ref, kn_ref, grp_ref, grpt_ref,
                    qk_ref, v_ref, r_ref, lr_ref, q1_ref, k1_ref, v1_ref,
                    q4_ref, k4_ref, v4_ref, q16_ref, k16_ref, v16_ref, sc_ref, sc4_ref):
    xn = _rms(x_ref[...], g_ref[...]).astype(BF16)
    c = 2 * GLA_QK + 2 * GLA_V
    qb = _dot(xn, w_ref[:, c:c + QK_DIL])
    qb = qb * _head_inv_rms(qb, grp_ref, grpt_ref) * qn_ref[...]
    _store_by_residue(qb, q1_ref, q4_ref, q16_ref, sc_ref.at[0], sc4_ref.at[0])
    c += QK_DIL
    kb = _dot(xn, w_ref[:, c:c + QK_DIL])
    kb = kb * _head_inv_rms(kb, grp_ref, grpt_ref) * kn_ref[...]
    _store_by_residue(kb, k1_ref, k4_ref, k16_ref, sc_ref.at[1], sc4_ref.at[1])
    c += QK_DIL
    _store_by_residue(_dot(xn, w_ref[:, c:c + QK_DIL]), v1_ref, v4_ref, v16_ref, sc_ref.at[2], sc4_ref.at[2])
    c += QK_DIL
    lr_ref[...] = _dot(xn, w_ref[:, c:c + LANES])
    c = 0
    for ref, width in ((qk_ref, 2 * GLA_QK), (v_ref, GLA_V)):
        ref[...] = _dot(xn, w_ref[:, c:c + width]).astype(BF16)
        c += width
    r_ref[...] = _silu(_dot(xn, w_ref[:, c:c + GLA_V])).astype(BF16)


def _in_even(x2d, gain, w, qn, kn, grp, grpt):
    t = x2d.shape[0]
    tm = ROW_TILE
    row = lambda width: pl.BlockSpec((tm, width), lambda i: (i, 0))
    res_spec = lambda d: pl.BlockSpec((d, tm // d, QK_DIL), lambda i: (0, i, 0))
    res_shape = lambda d: jax.ShapeDtypeStruct((d, t // d, QK_DIL), BF16)
    dils = [d for _, d in DIL_PATTERNS[1:]]
    out_shape = (
        jax.ShapeDtypeStruct((t, 2 * GLA_QK), BF16),
        jax.ShapeDtypeStruct((t, GLA_V), BF16),
        jax.ShapeDtypeStruct((t, GLA_V), BF16),
        jax.ShapeDtypeStruct((t, LANES), F32),
    ) + (jax.ShapeDtypeStruct((t, QK_DIL), BF16),) * 3 + tuple(res_shape(d) for d in dils for _ in range(3))
    return pl.pallas_call(
        _in_even_kernel,
        grid=(t // tm,),
        in_specs=[row(D_MODEL), _resident((1, D_MODEL)), _resident((D_MODEL, EVEN_W)),
                  _resident((1, QK_DIL)), _resident((1, QK_DIL)),
                  _resident((QK_DIL, LANES)), _resident((LANES, QK_DIL))],
        out_specs=[row(2 * GLA_QK), row(GLA_V), row(GLA_V), row(LANES)] + [row(QK_DIL)] * 3
                  + [res_spec(d) for d in dils for _ in range(3)],
        out_shape=out_shape,
        scratch_shapes=[pltpu.VMEM((3, QK_DIL // LANES, tm, LANES), F32),
                        pltpu.VMEM((3, DIL_STRIDE, QK_DIL // LANES, tm // DIL_STRIDE, LANES), F32)],
        compiler_params=_params(("parallel",)),
        name="in_even",
    )(x2d, gain, w, qn, kn, grp, grpt)


def _gla_kernel(qk_ref, v_ref, r_ref, lr_ref, wgu_ref, bg_ref, on_ref, tri_ref, o_ref, st_ref, oi_ref):
    c = GLA_CHUNK
    chunks = [slice(ci * c, (ci + 1) * c) for ci in range(GLA_STEP // c)]
    heads = [(slice(h * GLA_DK, (h + 1) * GLA_DK), slice(h * GLA_DV, (h + 1) * GLA_DV))
             for h in range(GLA_HEADS)]

    @pl.when(pl.program_id(1) == 0)
    def _():
        st_ref[...] = jnp.zeros_like(st_ref)

    z = _dot(lr_ref[...].astype(BF16), wgu_ref[...]) + bg_ref[...]
    log_a = _log_sigmoid(z) * (1.0 / GLA_TAU)
    hi, lo = _split(log_a)
    cum = _dot(tri_ref[...], hi) + _dot(tri_ref[...], lo)
    q = qk_ref[:, 0:GLA_QK].astype(F32) * GLA_DK ** -0.5
    k = qk_ref[:, GLA_QK:2 * GLA_QK].astype(F32)
    q_dec = (q * jnp.exp(cum)).astype(BF16)
    k_inv = (k * jnp.exp(-cum)).astype(BF16)
    totals = [cum[rs.stop - 1:rs.stop, :] for rs in chunks]
    k_end = [(k[rs] * jnp.exp(tot - cum[rs])).astype(BF16) for rs, tot in zip(chunks, totals)]
    e_tot = [jnp.exp(tot) for tot in totals]

    rows = lax.broadcasted_iota(jnp.int32, (c, c), 0)
    cols = lax.broadcasted_iota(jnp.int32, (c, c), 1)
    causal = rows >= cols
    for rs in chunks:
        for ks, vs in heads:
            att = jnp.where(causal, _dot_nt(q_dec[rs, ks], k_inv[rs, ks]), 0.0).astype(BF16)
            oi_ref[rs, vs] = _dot(att, v_ref[rs, vs])

    for ci, rs in enumerate(chunks):
        for h, (ks, vs) in enumerate(heads):
            st = st_ref[h]
            oi_ref[rs, vs] += _dot_nt(q_dec[rs, ks], st.astype(BF16))
            st_ref[h] = st * e_tot[ci][:, ks] + _dot_tn(v_ref[rs, vs], k_end[ci][:, ks])

    for _, vs in heads:
        o_ref[:, vs] = (_rms(oi_ref[:, vs], on_ref[...]) * r_ref[:, vs].astype(F32)).astype(BF16)


def _gla(qk, v, r, lr, wgu, bg, on, bsz, seq):
    t = qk.shape[0]
    ns = seq // GLA_STEP
    row = lambda width: pl.BlockSpec((GLA_STEP, width), lambda b, s: (b * ns + s, 0))
    pos = np.arange(GLA_STEP)
    same_chunk = pos[:, None] // GLA_CHUNK == pos[None, :] // GLA_CHUNK
    tri = jnp.asarray(same_chunk & (pos[:, None] >= pos[None, :]), BF16)
    return pl.pallas_call(
        _gla_kernel,
        grid=(bsz, ns),
        in_specs=[row(2 * GLA_QK), row(GLA_V), row(GLA_V), row(LANES),
                  _resident((LANES, GLA_QK)), _resident((1, GLA_QK)), _resident((1, GLA_DV)),
                  _resident((GLA_STEP, GLA_STEP))],
        out_specs=row(GLA_V),
        out_shape=jax.ShapeDtypeStruct((t, GLA_V), BF16),
        scratch_shapes=[pltpu.VMEM((GLA_HEADS, GLA_DV, GLA_DK), F32), pltpu.VMEM((GLA_STEP, GLA_V), F32)],
        compiler_params=_params(("parallel", "arbitrary")),
        name="gla",
    )(qk, v, r, lr, wgu, bg, on, tri)


def _dil_kernel(q_ref, k_ref, v_ref, e_ref, b_ref, a_ref, o_ref, lse_ref, kp_ref, vp_ref):
    blk = DIL_BLOCK
    n = pl.program_id(1)

    @pl.when(n == 0)
    def _():
        kp_ref[...] = jnp.zeros_like(kp_ref)
        vp_ref[...] = jnp.zeros_like(vp_ref)

    qi = lax.broadcasted_iota(jnp.int32, (2 * blk, 2 * blk), 0) & (blk - 1)
    kj = lax.broadcasted_iota(jnp.int32, (2 * blk, 2 * blk), 1)
    valid = ((kj < blk) & (kj >= qi) & (n > 0)) | ((kj >= blk) & (kj - blk <= qi))
    first = lax.broadcasted_iota(jnp.int32, (blk, LANES), 1) < DIL_HD
    ones = jnp.ones((2 * blk, LANES), BF16)
    for g in range(DIL_GROUPS):
        for c in range(QK_DIL // LANES):
            cs = slice(c * LANES, (c + 1) * LANES)
            q = q_ref[g, :, cs]
            zero = jnp.zeros_like(q)
            q2 = jnp.concatenate([jnp.where(first, q, zero), jnp.where(first, zero, q)], axis=0)
            q_aug = jnp.concatenate([q2, e_ref[...]], axis=1)
            k_aug = jnp.concatenate([jnp.concatenate([kp_ref[g, :, cs], k_ref[g, :, cs]], axis=0), b_ref[c]], axis=1)
            v_aug = jnp.concatenate([jnp.concatenate([vp_ref[g, :, cs], v_ref[g, :, cs]], axis=0), ones], axis=1)
            s = jnp.where(valid, _dot_nt(q_aug, k_aug), NEG_INF)
            m = jnp.max(s, axis=-1, keepdims=True)
            p = jnp.exp(s - m).astype(BF16)
            ov = _dot(p, v_aug)
            den = ov[:, LANES:]
            o2 = ov[:, :LANES] / den
            l2 = m + jnp.log(den) + a_ref[c]
            o_ref[g, :, cs] = jnp.where(first, o2[:blk], o2[blk:]).astype(BF16)
            lse_ref[g, :, cs] = jnp.where(first, l2[:blk], l2[blk:])
    kp_ref[...] = k_ref[...]
    vp_ref[...] = v_ref[...]


def _dilated_tables(dilation):
    blk = DIL_BLOCK
    slopes = np.exp2(-8.0 * np.arange(1, DIL_HEADS + 1) / DIL_HEADS)
    npair = QK_DIL // LANES
    e = np.zeros((2 * blk, LANES), np.float32)
    e[:blk, 0] = 1.0
    e[blk:, 1] = 1.0
    b = np.zeros((npair, 2 * blk, LANES), np.float32)
    a = np.zeros((npair, 2 * blk, LANES), np.float32)
    key_off = np.arange(2 * blk, dtype=np.float64) - blk
    qpos = np.arange(blk, dtype=np.float64)
    for c in range(npair):
        for half in range(2):
            sl = slopes[2 * c + half] * dilation
            b[c, :, half] = sl * key_off
            a[c, half * blk:(half + 1) * blk, :] = (-sl * qpos)[:, None]
    return jnp.asarray(e, BF16), jnp.asarray(b, BF16), jnp.asarray(a, F32)


def _dilated_pass(q, k, v, bsz, seq, dilation):
    seg = seq // dilation
    nb = seg // DIL_BLOCK
    ng = dilation * bsz
    assert ng % DIL_GROUPS == 0 and seg % DIL_BLOCK == 0
    e, b, a = _dilated_tables(dilation)
    blk = pl.BlockSpec((DIL_GROUPS, DIL_BLOCK, QK_DIL), lambda gi, n: (gi, n, 0))
    scratch = pltpu.VMEM((DIL_GROUPS, DIL_BLOCK, QK_DIL), BF16)
    return pl.pallas_call(
        _dil_kernel,
        grid=(ng // DIL_GROUPS, nb),
        in_specs=[blk, blk, blk, _resident(e.shape), _resident(b.shape), _resident(a.shape)],
        out_specs=[blk, blk],
        out_shape=(jax.ShapeDtypeStruct((ng, seg, QK_DIL), BF16), jax.ShapeDtypeStruct((ng, seg, QK_DIL), F32)),
        scratch_shapes=[scratch, scratch],
        compiler_params=_params(("parallel", "arbitrary")),
        name=f"dilated_d{dilation}",
    )(q, k, v, e, b, a)


def _to_natural(r4_ref, r16_ref, sc4_ref, sc16_ref, tmp_ref):
    nc = QK_DIL // LANES
    n4 = ROW_TILE // DIL_STRIDE
    n16 = n4 // DIL_STRIDE
    for b in range(DIL_STRIDE):
        for c in range(nc):
            cs = slice(c * LANES, (c + 1) * LANES)
            sc4_ref[c, pl.ds(b, n4, stride=DIL_STRIDE), :] = r4_ref[b, :, cs].astype(F32)
            for a in range(DIL_STRIDE):
                tmp_ref[b, c, pl.ds(a, n16, stride=DIL_STRIDE), :] = r16_ref[DIL_STRIDE * a + b, :, cs].astype(F32)
    for b in range(DIL_STRIDE):
        for c in range(nc):
            sc16_ref[c, pl.ds(b, n4, stride=DIL_STRIDE), :] = tmp_ref[b, c]


def _out_even_kernel(x_ref, oa_ref, o1_ref, l1_ref, o4_ref, l4_ref, o16_ref, l16_ref,
                     wa_ref, wb_ref, y_ref, so4_ref, sl4_ref, so16_ref, sl16_ref, to_ref, tl_ref, ob_ref):
    _to_natural(o4_ref, o16_ref, so4_ref, so16_ref, to_ref)
    _to_natural(l4_ref, l16_ref, sl4_ref, sl16_ref, tl_ref)
    for c in range(QK_DIL // LANES):
        cs = slice(c * LANES, (c + 1) * LANES)
        l1, l2, l3 = l1_ref[:, cs], sl4_ref[c], sl16_ref[c]
        m = jnp.maximum(jnp.maximum(l1, l2), l3)
        e1, e2, e3 = jnp.exp(l1 - m), jnp.exp(l2 - m), jnp.exp(l3 - m)
        ob = (e1 * o1_ref[:, cs].astype(F32) + e2 * so4_ref[c] + e3 * so16_ref[c]) / (e1 + e2 + e3)
        ob_ref[:, cs] = ob.astype(BF16)
    y_ref[...] = x_ref[...] + _dot(oa_ref[...], wa_ref[...]) + _dot(ob_ref[...], wb_ref[...])


def _out_even(x2d, oa, passes, wa, wb):
    t = x2d.shape[0]
    tm = ROW_TILE
    row = lambda width: pl.BlockSpec((tm, width), lambda i: (i, 0))
    res = lambda d: pl.BlockSpec((d, tm // d, QK_DIL), lambda i: (0, i, 0))
    (o1, l1), (o4, l4), (o16, l16) = passes
    d4, d16 = DIL_PATTERNS[1][1], DIL_PATTERNS[2][1]
    sc = pltpu.VMEM((QK_DIL // LANES, tm, LANES), F32)
    tmp = pltpu.VMEM((DIL_STRIDE, QK_DIL // LANES, tm // DIL_STRIDE, LANES), F32)
    return pl.pallas_call(
        _out_even_kernel,
        grid=(t // tm,),
        in_specs=[row(D_MODEL), row(GLA_V), row(QK_DIL), row(QK_DIL), res(d4), res(d4), res(d16), res(d16),
                  _resident((GLA_V, D_MODEL)), _resident((QK_DIL, D_MODEL))],
        out_specs=row(D_MODEL),
        out_shape=jax.ShapeDtypeStruct((t, D_MODEL), F32),
        scratch_shapes=[sc, sc, sc, sc, tmp, tmp, pltpu.VMEM((tm, QK_DIL), BF16)],
        compiler_params=_params(("parallel",)),
        name="out_even",
    )(x2d, oa, o1.reshape(t, QK_DIL), l1.reshape(t, QK_DIL), o4, l4, o16, l16, wa, wb)


def _ffn_kernel(x_ref, g_ref, wg_ref, wu_ref, wd_ref, y_ref, xn_ref, h_ref):
    x = x_ref[...]
    xn_ref[...] = _rms(x, g_ref[...]).astype(BF16)
    for ci in range(FFN_DENSE // FFN_CHUNK):
        cs = slice(ci * FFN_CHUNK, (ci + 1) * FFN_CHUNK)
        xn = xn_ref[...]
        h_ref[:, cs] = (_silu(_dot(xn, wg_ref[:, cs])) * _dot(xn, wu_ref[:, cs])).astype(BF16)
    y_ref[...] = x + _dot(h_ref[...], wd_ref[...])


def _ffn(x2d, gain, wg, wu, wd):
    t = x2d.shape[0]
    tm = ROW_TILE
    row = pl.BlockSpec((tm, D_MODEL), lambda i: (i, 0))
    return pl.pallas_call(
        _ffn_kernel,
        grid=(t // tm,),
        in_specs=[row, _resident((1, D_MODEL)), _resident((D_MODEL, FFN_DENSE)),
                  _resident((D_MODEL, FFN_DENSE)), _resident((FFN_DENSE, D_MODEL))],
        out_specs=row,
        out_shape=jax.ShapeDtypeStruct((t, D_MODEL), F32),
        scratch_shapes=[pltpu.VMEM((tm, D_MODEL), BF16), pltpu.VMEM((tm, FFN_DENSE), BF16)],
        compiler_params=_params(("parallel",)),
        name="ffn_dense",
    )(x2d, gain, wg, wu, wd)


def _in_odd_kernel(x_ref, g_ref, w_ref, h_ref):
    xn = _rms(x_ref[...], g_ref[...]).astype(BF16)
    width = RET_HEADS * RET_DK
    for ci in range(ODD_W // width):
        cs = slice(ci * width, (ci + 1) * width)
        acc = _dot(xn, w_ref[:, cs])
        if ci == 1:
            acc = acc * RET_DK ** -0.5
        if ci * width >= ODD_W - RET_HEADS * RET_DV:
            acc = _silu(acc)
        h_ref[:, cs] = acc.astype(BF16)


def _in_odd(x2d, gain, w):
    t = x2d.shape[0]
    tm = ROW_TILE
    return pl.pallas_call(
        _in_odd_kernel,
        grid=(t // tm,),
        in_specs=[pl.BlockSpec((tm, D_MODEL), lambda i: (i, 0)), _resident((1, D_MODEL)),
                  _resident((D_MODEL, ODD_W))],
        out_specs=pl.BlockSpec((tm, ODD_W), lambda i: (i, 0)),
        out_shape=jax.ShapeDtypeStruct((t, ODD_W), BF16),
        compiler_params=_params(("parallel",)),
        name="in_odd",
    )(x2d, gain, w)


def _ret_kernel(h_ref, dec_ref, qd_ref, kd_ref, cd_ref, gn_ref, o_ref, st_ref):
    @pl.when(pl.program_id(1) == 0)
    def _():
        st_ref[...] = jnp.zeros_like(st_ref)

    nk = RET_HEADS * RET_DK
    for h in range(RET_HEADS):
        ks = slice(h * RET_DK, (h + 1) * RET_DK)
        vs = slice(h * RET_DV, (h + 1) * RET_DV)
        q = h_ref[:, ks]
        k = h_ref[:, nk + h * RET_DK:nk + (h + 1) * RET_DK]
        v = h_ref[:, 2 * nk + h * RET_DV:2 * nk + (h + 1) * RET_DV]
        g = h_ref[:, 2 * nk + (RET_HEADS + h) * RET_DV:2 * nk + (RET_HEADS + h + 1) * RET_DV]
        att = (_dot_nt(q, k) * dec_ref[h]).astype(BF16)
        q_dec = (q.astype(F32) * qd_ref[h]).astype(BF16)
        k_dec = (k.astype(F32) * kd_ref[h]).astype(BF16)
        st = st_ref[h]
        o = _dot(att, v) + _dot(q_dec, st.astype(BF16))
        st_ref[h] = st * cd_ref[h] + _dot_tn(k_dec, v)
        mu = jnp.mean(o, axis=-1, keepdims=True)
        oc = o - mu
        var = jnp.mean(oc * oc, axis=-1, keepdims=True)
        o_ref[:, vs] = (oc * lax.rsqrt(var + GN_EPS) * gn_ref[:, vs] * g.astype(F32)).astype(BF16)


def _retention_tables():
    h, c = RET_HEADS, RET_CHUNK
    log_gamma = np.log1p(-np.exp2(-5.0 - np.arange(h, dtype=np.float64)))
    pos = np.arange(c, dtype=np.float64)
    rel = pos[:, None] - pos[None, :]
    decay = np.where(rel >= 0, np.exp(log_gamma[:, None, None] * np.maximum(rel, 0.0)), 0.0)
    q_dec = np.exp(log_gamma[:, None] * (pos + 1.0))[:, :, None]
    k_dec = np.exp(log_gamma[:, None] * (c - 1.0 - pos))[:, :, None]
    chunk_decay = np.broadcast_to(np.exp(log_gamma * c)[:, None, None], (h, 1, RET_DV))
    f = lambda a: jnp.asarray(np.ascontiguousarray(a), F32)
    return f(decay), f(q_dec), f(k_dec), f(chunk_decay)


def _retention(hcat, gn_w, bsz, seq):
    t = hcat.shape[0]
    c = RET_CHUNK
    ns = seq // c
    decay, q_dec, k_dec, chunk_decay = _retention_tables()
    return pl.pallas_call(
        _ret_kernel,
        grid=(bsz, ns),
        in_specs=[pl.BlockSpec((c, ODD_W), lambda b, s: (b * ns + s, 0)),
                  _resident((RET_HEADS, c, c)), _resident((RET_HEADS, c, 1)),
                  _resident((RET_HEADS, c, 1)), _resident((RET_HEADS, 1, RET_DV)),
                  _resident((1, RET_HEADS * RET_DV))],
        out_specs=pl.BlockSpec((c, RET_HEADS * RET_DV), lambda b, s: (b * ns + s, 0)),
        out_shape=jax.ShapeDtypeStruct((t, RET_HEADS * RET_DV), BF16),
        scratch_shapes=[pltpu.VMEM((RET_HEADS, RET_DK, RET_DV), F32)],
        compiler_params=_params(("parallel", "arbitrary")),
        name="retention",
    )(hcat, decay, q_dec, k_dec, chunk_decay, gn_w)


def _out_odd_kernel(x_ref, oc_ref, w_ref, g_ref, wr_ref, tri_ref,
                    y_ref, xn_ref, meta_ref, gate_ref, cnt_ref, run_ref):
    @pl.when(pl.program_id(0) == 0)
    def _():
        run_ref[...] = jnp.zeros_like(run_ref)

    y = x_ref[...] + _dot(oc_ref[...], w_ref[...])
    y_ref[...] = y
    xn = _rms(y, g_ref[...])
    for s in range(ROW_CHUNKS):
        xn_ref[_chunk_rows(s, ROW_TILE), :] = xn[:, s * LANES:(s + 1) * LANES]

    hi, lo = _split(xn)
    both = _dot(hi, wr_ref[...])
    logits = both[:, :LANES] + both[:, LANES:] + _dot(lo, wr_ref[:, :LANES])
    lane = lax.broadcasted_iota(jnp.int32, logits.shape, 1).astype(F32)
    logits = jnp.where(lane < N_EXPERTS, logits, -jnp.inf)
    t1 = jnp.max(logits, axis=-1, keepdims=True)
    i1 = jnp.min(jnp.where(logits == t1, lane, float(LANES)), axis=-1, keepdims=True)
    rest = jnp.where(lane == i1, -jnp.inf, logits)
    t2 = jnp.max(rest, axis=-1, keepdims=True)
    i2 = jnp.min(jnp.where(rest == t2, lane, float(LANES)), axis=-1, keepdims=True)
    e2 = jnp.exp(t2 - t1)
    den = 1.0 + e2
    gate_ref[...] = jnp.where(lane == 0.0, 1.0 / den, jnp.where(lane == 1.0, e2 / den, 0.0))

    sel = (lane == i1) | (lane == i2)
    rank = _dot(tri_ref[...], sel.astype(BF16)) + run_ref[...]
    r1 = jnp.sum(jnp.where(lane == i1, rank, 0.0), axis=-1, keepdims=True)
    r2 = jnp.sum(jnp.where(lane == i2, rank, 0.0), axis=-1, keepdims=True)
    run_ref[...] += jnp.sum(sel.astype(F32), axis=0, keepdims=True)
    cnt_ref[...] = run_ref[...]
    meta = jnp.where(lane == 0.0, i1, jnp.where(lane == 1.0, i2,
                     jnp.where(lane == 2.0, r1, jnp.where(lane == 3.0, r2, 0.0))))
    meta_ref[...] = meta.astype(jnp.int32)


def _out_odd(x2d, oc, w, gain, wr):
    t = x2d.shape[0]
    tm = ROW_TILE
    tri = jnp.asarray(np.tril(np.ones((tm, tm), np.float32), -1), BF16)
    row = lambda width: pl.BlockSpec((tm, width), lambda i: (i, 0))
    return pl.pallas_call(
        _out_odd_kernel,
        grid=(t // tm,),
        in_specs=[row(D_MODEL), row(RET_HEADS * RET_DV), _resident((RET_HEADS * RET_DV, D_MODEL)),
                  _resident((1, D_MODEL)), _resident((D_MODEL, 2 * LANES)), _resident((tm, tm))],
        out_specs=[row(D_MODEL), pl.BlockSpec((tm * ROW_CHUNKS, LANES), lambda i: (i, 0)),
                   row(LANES), row(LANES), pl.BlockSpec((1, LANES), lambda i: (0, 0))],
        out_shape=(jax.ShapeDtypeStruct((t, D_MODEL), F32),
                   jax.ShapeDtypeStruct((t * ROW_CHUNKS, LANES), F32),
                   jax.ShapeDtypeStruct((t, LANES), jnp.int32),
                   jax.ShapeDtypeStruct((t, LANES), F32),
                   jax.ShapeDtypeStruct((1, LANES), F32)),
        scratch_shapes=[pltpu.VMEM((1, LANES), F32)],
        compiler_params=_params(("arbitrary",)),
        name="out_odd_router",
    )(x2d, oc, w, gain, wr, tri)


def _chunk_rows(s, n_rows):
    return pl.ds(s, n_rows, stride=ROW_CHUNKS)


def _tile_row(ref, row):
    return ref.at[pl.ds(pl.multiple_of(row * ROW_CHUNKS, ROW_CHUNKS), ROW_CHUNKS), :]


def _row_copy(src_ref, src_row, dst_ref, dst_row, sem):
    return pltpu.make_async_copy(_tile_row(src_ref, src_row), _tile_row(dst_ref, dst_row), sem)


def _dispatch_kernel(zs_ref, zon_ref, dest_ref, x_ref, xs_ref, zero_ref, sem):
    n = DISPATCH_ROWS

    @pl.when(pl.program_id(0) == 0)
    def _():
        zero_ref[...] = jnp.zeros_like(zero_ref)
        blocks = [pltpu.make_async_copy(
            zero_ref, xs_ref.at[pl.ds(pl.multiple_of(zs_ref[e] * ROW_CHUNKS, ROW_CHUNKS),
                                      MOE_ROWS * ROW_CHUNKS), :], sem) for e in range(2 * N_EXPERTS)]
        for e, cp in enumerate(blocks):
            pl.when(zon_ref[e] > 0)(cp.start)
        for e, cp in enumerate(blocks):
            pl.when(zon_ref[e] > 0)(cp.wait)

    def issue(j, carry):
        _row_copy(x_ref, j, xs_ref, dest_ref[0, 0, j], sem).start()
        _row_copy(x_ref, j, xs_ref, dest_ref[0, 0, n + j], sem).start()
        return carry

    lax.fori_loop(0, n, issue, 0, unroll=8)

    def drain(j, carry):
        _row_copy(x_ref, 0, xs_ref, 0, sem).wait()
        return carry

    lax.fori_loop(0, 2 * n, drain, 0, unroll=8)


def _dispatch(xn3, dest_blocks, zero_starts, zero_on, n_rows):
    t = xn3.shape[0] // ROW_CHUNKS
    n = DISPATCH_ROWS
    grid_spec = pltpu.PrefetchScalarGridSpec(
        num_scalar_prefetch=2,
        grid=(t // n,),
        in_specs=[pl.BlockSpec((1, 1, 2 * n), lambda i, zs, zon: (i, 0, 0), memory_space=pltpu.SMEM),
                  pl.BlockSpec((n * ROW_CHUNKS, LANES), lambda i, zs, zon: (i, 0))],
        out_specs=pl.BlockSpec(memory_space=pl.ANY),
        scratch_shapes=[pltpu.VMEM((MOE_ROWS * ROW_CHUNKS, LANES), F32), pltpu.SemaphoreType.DMA(())],
    )
    return pl.pallas_call(
        _dispatch_kernel,
        grid_spec=grid_spec,
        out_shape=jax.ShapeDtypeStruct((n_rows * ROW_CHUNKS, LANES), F32),
        compiler_params=_params(("arbitrary",)),
        name="moe_dispatch",
    )(zero_starts, zero_on, dest_blocks, xn3)


def _moe_kernel(be_ref, nv_ref, x_ref, wg_ref, wu_ref, wd_ref, y_ref, xb_ref, h_ref, acc_ref):
    del be_ref
    b = pl.program_id(0)
    f = pl.program_id(1)
    nf = pl.num_programs(1)

    valid = b < nv_ref[0]

    @pl.when(f == 0)
    def _():
        acc_ref[...] = jnp.zeros_like(acc_ref)

    @pl.when(valid & (f == 0))
    def _():
        for s in range(ROW_CHUNKS):
            xb_ref[:, s * LANES:(s + 1) * LANES] = x_ref[_chunk_rows(s, MOE_ROWS), :].astype(BF16)

    @pl.when(valid)
    def _():
        for ci in range(MOE_FCHUNK // MOE_INNER):
            cs = slice(ci * MOE_INNER, (ci + 1) * MOE_INNER)
            x = xb_ref[...]
            h_ref[:, cs] = (_silu(_dot(x, wg_ref[0, :, cs])) * _dot(x, wu_ref[0, :, cs])).astype(BF16)
        acc_ref[...] += _dot(h_ref[...], wd_ref[0])

    @pl.when(f == nf - 1)
    def _():
        for s in range(ROW_CHUNKS):
            y_ref[_chunk_rows(s, MOE_ROWS), :] = acc_ref[:, s * LANES:(s + 1) * LANES]


def _moe_experts(xs, block_expert, n_valid, wg, wu, wd):
    n_rows = xs.shape[0] // ROW_CHUNKS
    bm, tf = MOE_ROWS, MOE_FCHUNK
    nf = FFN_EXPERT // tf
    fidx = lambda b, f, nv: jnp.where(b < nv[0], f, nf - 1)
    rows = pl.BlockSpec((bm * ROW_CHUNKS, LANES), lambda b, f, be, nv: (b, 0))
    grid_spec = pltpu.PrefetchScalarGridSpec(
        num_scalar_prefetch=2,
        grid=(n_rows // bm, nf),
        in_specs=[rows,
                  pl.BlockSpec((1, D_MODEL, tf), lambda b, f, be, nv: (be[b], 0, fidx(b, f, nv))),
                  pl.BlockSpec((1, D_MODEL, tf), lambda b, f, be, nv: (be[b], 0, fidx(b, f, nv))),
                  pl.BlockSpec((1, tf, D_MODEL), lambda b, f, be, nv: (be[b], fidx(b, f, nv), 0))],
        out_specs=rows,
        scratch_shapes=[pltpu.VMEM((bm, D_MODEL), BF16), pltpu.VMEM((bm, tf), BF16),
                        pltpu.VMEM((bm, D_MODEL), F32)],
    )
    return pl.pallas_call(
        _moe_kernel,
        grid_spec=grid_spec,
        out_shape=jax.ShapeDtypeStruct((n_rows * ROW_CHUNKS, LANES), F32),
        compiler_params=_params(("arbitrary", "arbitrary")),
        name="moe_experts",
    )(block_expert, n_valid, xs, wg, wu, wd)


def _combine_kernel(dcur_ref, dnext_ref, x_ref, gate_ref, y_ref, o_ref, buf_ref, sem):
    n = GATHER_ROWS
    i = pl.program_id(0)
    slot = i % 2

    def issue(dref, sl):
        def body(j, carry):
            _row_copy(y_ref, dref[0, 0, j], buf_ref.at[sl], j, sem.at[sl]).start()
            return carry
        lax.fori_loop(0, 2 * n, body, 0, unroll=8)

    @pl.when(i == 0)
    def _():
        issue(dcur_ref, 0)

    @pl.when(i + 1 < pl.num_programs(0))
    def _():
        issue(dnext_ref, 1 - slot)

    def drain(j, carry):
        _row_copy(y_ref, 0, buf_ref.at[slot], 0, sem.at[slot]).wait()
        return carry

    lax.fori_loop(0, 2 * n, drain, 0, unroll=8)

    g1 = gate_ref[:, 0:1]
    g2 = gate_ref[:, 1:2]
    for s in range(ROW_CHUNKS):
        cs = slice(s * LANES, (s + 1) * LANES)
        o_ref[:, cs] = (x_ref[:, cs] + g1 * buf_ref[slot, _chunk_rows(s, n), :]
                        + g2 * buf_ref[slot, _chunk_rows(n * ROW_CHUNKS + s, n), :])


def _combine(x3, gates, ys, dest_blocks):
    t = x3.shape[0]
    n = GATHER_ROWS
    nt = t // n
    idx = lambda f: pl.BlockSpec((1, 1, 2 * n), f, memory_space=pltpu.SMEM)
    return pl.pallas_call(
        _combine_kernel,
        grid=(nt,),
        in_specs=[idx(lambda i: (i, 0, 0)), idx(lambda i: (jnp.minimum(i + 1, nt - 1), 0, 0)),
                  pl.BlockSpec((n, D_MODEL), lambda i: (i, 0)),
                  pl.BlockSpec((n, LANES), lambda i: (i, 0)),
                  pl.BlockSpec(memory_space=pl.ANY)],
        out_specs=pl.BlockSpec((n, D_MODEL), lambda i: (i, 0)),
        out_shape=jax.ShapeDtypeStruct((t, D_MODEL), F32),
        scratch_shapes=[pltpu.VMEM((2, 2 * n * ROW_CHUNKS, LANES), F32),
                        pltpu.SemaphoreType.DMA((2,))],
        compiler_params=_params(("arbitrary",)),
        name="moe_combine",
    )(dest_blocks, dest_blocks, x3, gates, ys)


def _even_weights(w_in, q_norm, k_norm):
    a = 2 * GLA_QK + 2 * GLA_V
    pad = jnp.zeros((D_MODEL, LANES - GLA_RANK), w_in.dtype)
    w = jnp.concatenate([w_in[:, :a], w_in[:, a + GLA_RANK:], w_in[:, a:a + GLA_RANK], pad], axis=1)
    qn = (jnp.tile(q_norm, DIL_HEADS) * DIL_HD ** -0.5).reshape(1, QK_DIL)
    kn = jnp.tile(k_norm, DIL_HEADS).reshape(1, QK_DIL)
    grp = np.zeros((QK_DIL, LANES), np.float32)
    grp[np.arange(QK_DIL), np.arange(QK_DIL) // DIL_HD] = 1.0
    return (w.astype(BF16), qn.astype(F32), kn.astype(F32),
            jnp.asarray(grp, BF16), jnp.asarray(grp.T.copy(), BF16))


def kernel(x, mix_norm_even, w_in_even, w_gla_gate_up, b_gla_gate, gla_out_norm, q_norm, k_norm,
           w_out_even, ffn_norm_even, w_ffn_gate, w_ffn_up, w_ffn_down, mix_norm_odd, w_in_odd,
           ret_group_norm, w_out_odd, ffn_norm_odd, w_router, w_exp_gate, w_exp_up, w_exp_down):
    bsz, seq, d = x.shape
    t = bsz * seq
    x2d = x.reshape(t, d)
    row_vec = lambda v: v.reshape(1, -1).astype(F32)

    w_in, qn, kn, grp, grpt = _even_weights(w_in_even[0], q_norm[0], k_norm[0])
    (qk_a, v_a, r_a, lr, q1, k1, v1, q4, k4, v4, q16, k16, v16) = _in_even(
        x2d, row_vec(mix_norm_even[0]), w_in, qn, kn, grp, grpt)
    wgu = jnp.zeros((LANES, GLA_QK), F32).at[:GLA_RANK].set(w_gla_gate_up[0]).astype(BF16)
    o_a = _gla(qk_a, v_a, r_a, lr, wgu, row_vec(b_gla_gate[0]), row_vec(gla_out_norm[0]), bsz, seq)
    seqs = lambda a, dil: a.reshape(dil * bsz, seq // dil, QK_DIL)
    passes = [[o.reshape(dil, t // dil, QK_DIL) for o in
               _dilated_pass(seqs(q, dil), seqs(k, dil), seqs(v, dil), bsz, seq, dil)]
              for (_, dil), (q, k, v) in zip(DIL_PATTERNS, ((q1, k1, v1), (q4, k4, v4), (q16, k16, v16)))]
    w_out = w_out_even[0].astype(BF16)
    x2d = _out_even(x2d, o_a, passes, w_out[:GLA_V], w_out[GLA_V:])
    x2d = _ffn(x2d, row_vec(ffn_norm_even[0]),
               w_ffn_gate[0].astype(BF16), w_ffn_up[0].astype(BF16), w_ffn_down[0].astype(BF16))

    hcat = _in_odd(x2d, row_vec(mix_norm_odd[0]), w_in_odd[0].astype(BF16))
    o_c = _retention(hcat, row_vec(ret_group_norm[0]), bsz, seq)
    wr = jnp.zeros((D_MODEL, LANES), F32).at[:, :N_EXPERTS].set(w_router[0])
    wr_hi = wr.astype(BF16)
    wr_lo = (wr - wr_hi.astype(F32)).astype(BF16)
    x3, xn3, meta, gates, counts = _out_odd(x2d, o_c, w_out_odd[0].astype(BF16), row_vec(ffn_norm_odd[0]),
                                            jnp.concatenate([wr_hi, wr_lo], axis=1))

    bm = MOE_ROWS
    n_blocks = t * 2 // bm + N_EXPERTS
    n_rows = n_blocks * bm
    cnt = counts[0, :N_EXPERTS].astype(jnp.int32)
    padded = (cnt + bm - 1) // bm * bm
    pends = jnp.cumsum(padded)
    pstarts = pends - padded
    dest1 = pstarts[meta[:, 0]] + meta[:, 2]
    dest2 = pstarts[meta[:, 1]] + meta[:, 3]

    def dest_blocks(n):
        return jnp.concatenate([dest1.reshape(t // n, n), dest2.reshape(t // n, n)],
                               axis=1).reshape(t // n, 1, 2 * n).astype(jnp.int32)

    block_expert = jnp.minimum(
        jnp.searchsorted(pends, jnp.arange(n_blocks, dtype=jnp.int32) * bm, side='right'),
        N_EXPERTS - 1).astype(jnp.int32)
    n_valid = (pends[-1:] // bm).astype(jnp.int32)

    tail = pends[-1] + jnp.arange(N_EXPERTS, dtype=jnp.int32) * bm
    zero_on = jnp.concatenate([padded > 0, tail + bm <= n_rows]).astype(jnp.int32)
    zero_starts = jnp.concatenate([jnp.maximum(pends - bm, 0), jnp.minimum(tail, n_rows - bm)]).astype(jnp.int32)
    xs = _dispatch(xn3, dest_blocks(DISPATCH_ROWS), zero_starts, zero_on, n_rows)
    ys = _moe_experts(xs, block_expert, n_valid, w_exp_gate[0].astype(BF16),
                      w_exp_up[0].astype(BF16), w_exp_down[0].astype(BF16))
    out = _combine(x3, gates, ys, dest_blocks(GATHER_ROWS))
    return out.reshape(bsz, seq, d)
```

```python
import numpy as np
import jax
import jax.numpy as jnp
from jax import lax
from jax.experimental import pallas as pl
from jax.experimental.pallas import tpu as pltpu

BF16 = jnp.bfloat16
F32 = jnp.float32

D_MODEL = 1024
NORM_EPS = 1e-6
GN_EPS = 1e-5
NEG_INF = -1e30

GLA_HEADS, GLA_DK, GLA_DV, GLA_RANK, GLA_TAU, GLA_CHUNK = 4, 128, 256, 16, 16.0, 64
DIL_HEADS, DIL_HD, DIL_BLOCK = 8, 64, 128
DIL_PATTERNS = ((128, 1), (512, 4), (2048, 16))
RET_HEADS, RET_DK, RET_DV = 4, 256, 512
FFN_DENSE = 2816
N_EXPERTS, FFN_EXPERT = 8, 3584

LANES = 128
SUBLANES = 8
ROW_CHUNKS = D_MODEL // LANES
VMEM_LIMIT = 56 * 1024 * 1024

ROW_TILE = 512
GLA_STEP = 256
RET_CHUNK = 256
FFN_CHUNK = 256
DIL_GROUPS = 4
MOE_ROWS = 512
MOE_FCHUNK = 1792
MOE_INNER = 256
GATHER_ROWS = 512
DISPATCH_ROWS = 1024

QK_DIL = DIL_HEADS * DIL_HD
GLA_QK = GLA_HEADS * GLA_DK
GLA_V = GLA_HEADS * GLA_DV
EVEN_W = 2 * GLA_QK + 2 * GLA_V + 3 * QK_DIL + LANES
ODD_W = 2 * RET_HEADS * RET_DK + 2 * RET_HEADS * RET_DV


def _dot(a, b):
    return jnp.dot(a, b, preferred_element_type=F32)


def _dot_nt(a, b):
    return lax.dot_general(a, b, (((1,), (1,)), ((), ())), preferred_element_type=F32)


def _dot_tn(a, b):
    return lax.dot_general(a, b, (((0,), (0,)), ((), ())), preferred_element_type=F32)


def _split(x):
    hi = x.astype(BF16)
    lo = (x - hi.astype(F32)).astype(BF16)
    return hi, lo


def _rms(x, gain, eps=NORM_EPS):
    ms = jnp.mean(x * x, axis=-1, keepdims=True)
    return x * lax.rsqrt(ms + eps) * gain


def _silu(x):
    return x / (1.0 + jnp.exp(-x))


def _log_sigmoid(z):
    return jnp.minimum(z, 0.0) - jnp.log1p(jnp.exp(-jnp.abs(z)))


def _params(sem):
    return pltpu.CompilerParams(dimension_semantics=sem, vmem_limit_bytes=VMEM_LIMIT)


def _resident(shape):
    nd = len(shape)
    return pl.BlockSpec(shape, lambda *_: (0,) * nd, pipeline_mode=pl.Buffered(1))


def _head_inv_rms(acc, grp_ref, grpt_ref):
    ss = _dot((acc * acc).astype(BF16), grp_ref[...])
    inv = lax.rsqrt(ss * (1.0 / DIL_HD) + NORM_EPS)
    ih, il = _split(inv)
    return _dot(ih, grpt_ref[...]) + _dot(il, grpt_ref[...])


DIL_STRIDE = 4
assert [d for _, d in DIL_PATTERNS] == [1, DIL_STRIDE, DIL_STRIDE * DIL_STRIDE]


def _store_by_residue(val, nat_ref, r4_ref, r16_ref, sc_ref, sc4_ref):
    nat_ref[...] = val.astype(BF16)
    nc = QK_DIL // LANES
    n4 = ROW_TILE // DIL_STRIDE
    n16 = n4 // DIL_STRIDE
    for c in range(nc):
        sc_ref[c] = val[:, c * LANES:(c + 1) * LANES]
    for b in range(DIL_STRIDE):
        for c in range(nc):
            y = sc_ref[c, pl.ds(b, n4, stride=DIL_STRIDE), :]
            r4_ref[b, :, c * LANES:(c + 1) * LANES] = y.astype(BF16)
            sc4_ref[b, c] = y
    for b in range(DIL_STRIDE):
        for a in range(DIL_STRIDE):
            for c in range(nc):
                r16_ref[DIL_STRIDE * a + b, :, c * LANES:(c + 1) * LANES] = (
                    sc4_ref[b, c, pl.ds(a, n16, stride=DIL_STRIDE), :].astype(BF16))


def _in_even_kernel(x_ref, g_ref, w_ref, qn_ref, kn_ref, grp_ref, grpt_ref,
                    qk_ref, v_ref, r_ref, lr_ref, q1_ref, k1_ref, v1_ref,
                    q4_ref, k4_ref, v4_ref, q16_ref, k16_ref, v16_ref, sc_ref, sc4_ref):
    xn = _rms(x_ref[...], g_ref[...]).astype(BF16)
    c = 2 * GLA_QK + 2 * GLA_V
    qb = _dot(xn, w_ref[:, c:c + QK_DIL])
    qb = qb * _head_inv_rms(qb, grp_ref, grpt_ref) * qn_ref[...]
    _store_by_residue(qb, q1_ref, q4_ref, q16_ref, sc_ref.at[0], sc4_ref.at[0])
    c += QK_DIL
    kb = _dot(xn, w_ref[:, c:c + QK_DIL])
    kb = kb * _head_inv_rms(kb, grp_ref, grpt_ref) * kn_ref[...]
    _store_by_residue(kb, k1_ref, k4_ref, k16_ref, sc_ref.at[1], sc4_ref.at[1])
    c += QK_DIL
    _store_by_residue(_dot(xn, w_ref[:, c:c + QK_DIL]), v1_ref, v4_ref, v16_ref, sc_ref.at[2], sc4_ref.at[2])
    c += QK_DIL
    lr_ref[...] = _dot(xn, w_ref[:, c:c + LANES])
    c = 0
    for ref, width in ((qk_ref, 2 * GLA_QK), (v_ref, GLA_V)):
        ref[...] = _dot(xn, w_ref[:, c:c + width]).astype(BF16)
        c += width
    r_ref[...] = _silu(_dot(xn, w_ref[:, c:c + GLA_V])).astype(BF16)


def _in_even(x2d, gain, w, qn, kn, grp, grpt):
    t = x2d.shape[0]
    tm = ROW_TILE
    row = lambda width: pl.BlockSpec((tm, width), lambda i: (i, 0))
    res_spec = lambda d: pl.BlockSpec((d, tm // d, QK_DIL), lambda i: (0, i, 0))
    res_shape = lambda d: jax.ShapeDtypeStruct((d, t // d, QK_DIL), BF16)
    dils = [d for _, d in DIL_PATTERNS[1:]]
    out_shape = (
        jax.ShapeDtypeStruct((t, 2 * GLA_QK), BF16),
        jax.ShapeDtypeStruct((t, GLA_V), BF16),
        jax.ShapeDtypeStruct((t, GLA_V), BF16),
        jax.ShapeDtypeStruct((t, LANES), F32),
    ) + (jax.ShapeDtypeStruct((t, QK_DIL), BF16),) * 3 + tuple(res_shape(d) for d in dils for _ in range(3))
    return pl.pallas_call(
        _in_even_kernel,
        grid=(t // tm,),
        in_specs=[row(D_MODEL), _resident((1, D_MODEL)), _resident((D_MODEL, EVEN_W)),
                  _resident((1, QK_DIL)), _resident((1, QK_DIL)),
                  _resident((QK_DIL, LANES)), _resident((LANES, QK_DIL))],
        out_specs=[row(2 * GLA_QK), row(GLA_V), row(GLA_V), row(LANES)] + [row(QK_DIL)] * 3
                  + [res_spec(d) for d in dils for _ in range(3)],
        out_shape=out_shape,
        scratch_shapes=[pltpu.VMEM((3, QK_DIL // LANES, tm, LANES), F32),
                        pltpu.VMEM((3, DIL_STRIDE, QK_DIL // LANES, tm // DIL_STRIDE, LANES), F32)],
        compiler_params=_params(("parallel",)),
        name="in_even",
    )(x2d, gain, w, qn, kn, grp, grpt)


def _gla_kernel(qk_ref, v_ref, r_ref, lr_ref, wgu_ref, bg_ref, on_ref, tri_ref, o_ref, st_ref, oi_ref):
    c = GLA_CHUNK
    chunks = [slice(ci * c, (ci + 1) * c) for ci in range(GLA_STEP // c)]
    heads = [(slice(h * GLA_DK, (h + 1) * GLA_DK), slice(h * GLA_DV, (h + 1) * GLA_DV))
             for h in range(GLA_HEADS)]

    @pl.when(pl.program_id(1) == 0)
    def _():
        st_ref[...] = jnp.zeros_like(st_ref)

    z = _dot(lr_ref[...].astype(BF16), wgu_ref[...]) + bg_ref[...]
    log_a = _log_sigmoid(z) * (1.0 / GLA_TAU)
    hi, lo = _split(log_a)
    cum = _dot(tri_ref[...], hi) + _dot(tri_ref[...], lo)
    q = qk_ref[:, 0:GLA_QK].astype(F32) * GLA_DK ** -0.5
    k = qk_ref[:, GLA_QK:2 * GLA_QK].astype(F32)
    q_dec = (q * jnp.exp(cum)).astype(BF16)
    k_inv = (k * jnp.exp(-cum)).astype(BF16)
    totals = [cum[rs.stop - 1:rs.stop, :] for rs in chunks]
    k_end = [(k[rs] * jnp.exp(tot - cum[rs])).astype(BF16) for rs, tot in zip(chunks, totals)]
    e_tot = [jnp.exp(tot) for tot in totals]

    rows = lax.broadcasted_iota(jnp.int32, (c, c), 0)
    cols = lax.broadcasted_iota(jnp.int32, (c, c), 1)
    causal = rows >= cols
    for rs in chunks:
        for ks, vs in heads:
            att = jnp.where(causal, _dot_nt(q_dec[rs, ks], k_inv[rs, ks]), 0.0).astype(BF16)
            oi_ref[rs, vs] = _dot(att, v_ref[rs, vs])

    for ci, rs in enumerate(chunks):
        for h, (ks, vs) in enumerate(heads):
            st = st_ref[h]
            oi_ref[rs, vs] += _dot_nt(q_dec[rs, ks], st.astype(BF16))
            st_ref[h] = st * e_tot[ci][:, ks] + _dot_tn(v_ref[rs, vs], k_end[ci][:, ks])

    for _, vs in heads:
        o_ref[:, vs] = (_rms(oi_ref[:, vs], on_ref[...]) * r_ref[:, vs].astype(F32)).astype(BF16)


def _gla(qk, v, r, lr, wgu, bg, on, bsz, seq):
    t = qk.shape[0]
    ns = seq // GLA_STEP
    row = lambda width: pl.BlockSpec((GLA_STEP, width), lambda b, s: (b * ns + s, 0))
    pos = np.arange(GLA_STEP)
    same_chunk = pos[:, None] // GLA_CHUNK == pos[None, :] // GLA_CHUNK
    tri = jnp.asarray(same_chunk & (pos[:, None] >= pos[None, :]), BF16)
    return pl.pallas_call(
        _gla_kernel,
        grid=(bsz, ns),
        in_specs=[row(2 * GLA_QK), row(GLA_V), row(GLA_V), row(LANES),
                  _resident((LANES, GLA_QK)), _resident((1, GLA_QK)), _resident((1, GLA_DV)),
                  _resident((GLA_STEP, GLA_STEP))],
        out_specs=row(GLA_V),
        out_shape=jax.ShapeDtypeStruct((t, GLA_V), BF16),
        scratch_shapes=[pltpu.VMEM((GLA_HEADS, GLA_DV, GLA_DK), F32), pltpu.VMEM((GLA_STEP, GLA_V), F32)],
        compiler_params=_params(("parallel", "arbitrary")),
        name="gla",
    )(qk, v, r, lr, wgu, bg, on, tri)


def _dil_kernel(q_ref, k_ref, v_ref, e_ref, b_ref, a_ref, o_ref, lse_ref, kp_ref, vp_ref):
    blk = DIL_BLOCK
    n = pl.program_id(1)

    @pl.when(n == 0)
    def _():
        kp_ref[...] = jnp.zeros_like(kp_ref)
        vp_ref[...] = jnp.zeros_like(vp_ref)

    qi = lax.broadcasted_iota(jnp.int32, (2 * blk, 2 * blk), 0) & (blk - 1)
    kj = lax.broadcasted_iota(jnp.int32, (2 * blk, 2 * blk), 1)
    valid = ((kj < blk) & (kj >= qi) & (n > 0)) | ((kj >= blk) & (kj - blk <= qi))
    first = lax.broadcasted_iota(jnp.int32, (blk, LANES), 1) < DIL_HD
    ones = jnp.ones((2 * blk, LANES), BF16)
    for g in range(DIL_GROUPS):
        for c in range(QK_DIL // LANES):
            cs = slice(c * LANES, (c + 1) * LANES)
            q = q_ref[g, :, cs]
            zero = jnp.zeros_like(q)
            q2 = jnp.concatenate([jnp.where(first, q, zero), jnp.where(first, zero, q)], axis=0)
            q_aug = jnp.concatenate([q2, e_ref[...]], axis=1)
            k_aug = jnp.concatenate([jnp.concatenate([kp_ref[g, :, cs], k_ref[g, :, cs]], axis=0), b_ref[c]], axis=1)
            v_aug = jnp.concatenate([jnp.concatenate([vp_ref[g, :, cs], v_ref[g, :, cs]], axis=0), ones], axis=1)
            s = jnp.where(valid, _dot_nt(q_aug, k_aug), NEG_INF)
            m = jnp.max(s, axis=-1, keepdims=True)
            p = jnp.exp(s - m).astype(BF16)
            ov = _dot(p, v_aug)
            den = ov[:, LANES:]
            o2 = ov[:, :LANES] / den
            l2 = m + jnp.log(den) + a_ref[c]
            o_ref[g, :, cs] = jnp.where(first, o2[:blk], o2[blk:]).astype(BF16)
            lse_ref[g, :, cs] = jnp.where(first, l2[:blk], l2[blk:])
    kp_ref[...] = k_ref[...]
    vp_ref[...] = v_ref[...]


def _dilated_tables(dilation):
    blk = DIL_BLOCK
    slopes = np.exp2(-8.0 * np.arange(1, DIL_HEADS + 1) / DIL_HEADS)
    npair = QK_DIL // LANES
    e = np.zeros((2 * blk, LANES), np.float32)
    e[:blk, 0] = 1.0
    e[blk:, 1] = 1.0
    b = np.zeros((npair, 2 * blk, LANES), np.float32)
    a = np.zeros((npair, 2 * blk, LANES), np.float32)
    key_off = np.arange(2 * blk, dtype=np.float64) - blk
    qpos = np.arange(blk, dtype=np.float64)
    for c in range(npair):
        for half in range(2):
            sl = slopes[2 * c + half] * dilation
            b[c, :, half] = sl * key_off
            a[c, half * blk:(half + 1) * blk, :] = (-sl * qpos)[:, None]
    return jnp.asarray(e, BF16), jnp.asarray(b, BF16), jnp.asarray(a, F32)


def _dilated_pass(q, k, v, bsz, seq, dilation):
    seg = seq // dilation
    nb = seg // DIL_BLOCK
    ng = dilation * bsz
    assert ng % DIL_GROUPS == 0 and seg % DIL_BLOCK == 0
    e, b, a = _dilated_tables(dilation)
    blk = pl.BlockSpec((DIL_GROUPS, DIL_BLOCK, QK_DIL), lambda gi, n: (gi, n, 0))
    scratch = pltpu.VMEM((DIL_GROUPS, DIL_BLOCK, QK_DIL), BF16)
    return pl.pallas_call(
        _dil_kernel,
        grid=(ng // DIL_GROUPS, nb),
        in_specs=[blk, blk, blk, _resident(e.shape), _resident(b.shape), _resident(a.shape)],
        out_specs=[blk, blk],
        out_shape=(jax.ShapeDtypeStruct((ng, seg, QK_DIL), BF16), jax.ShapeDtypeStruct((ng, seg, QK_DIL), F32)),
        scratch_shapes=[scratch, scratch],
        compiler_params=_params(("parallel", "arbitrary")),
        name=f"dilated_d{dilation}",
    )(q, k, v, e, b, a)


def _to_natural(r4_ref, r16_ref, sc4_ref, sc16_ref, tmp_ref):
    nc = QK_DIL // LANES
    n4 = ROW_TILE // DIL_STRIDE
    n16 = n4 // DIL_STRIDE
    for b in range(DIL_STRIDE):
        for c in range(nc):
            cs = slice(c * LANES, (c + 1) * LANES)
            sc4_ref[c, pl.ds(b, n4, stride=DIL_STRIDE), :] = r4_ref[b, :, cs].astype(F32)
            for a in range(DIL_STRIDE):
                tmp_ref[b, c, pl.ds(a, n16, stride=DIL_STRIDE), :] = r16_ref[DIL_STRIDE * a + b, :, cs].astype(F32)
    for b in range(DIL_STRIDE):
        for c in range(nc):
            sc16_ref[c, pl.ds(b, n4, stride=DIL_STRIDE), :] = tmp_ref[b, c]


def _out_even_kernel(x_ref, oa_ref, o1_ref, l1_ref, o4_ref, l4_ref, o16_ref, l16_ref,
                     wa_ref, wb_ref, y_ref, so4_ref, sl4_ref, so16_ref, sl16_ref, to_ref, tl_ref, ob_ref):
    _to_natural(o4_ref, o16_ref, so4_ref, so16_ref, to_ref)
    _to_natural(l4_ref, l16_ref, sl4_ref, sl16_ref, tl_ref)
    for c in range(QK_DIL // LANES):
        cs = slice(c * LANES, (c + 1) * LANES)
        l1, l2, l3 = l1_ref[:, cs], sl4_ref[c], sl16_ref[c]
        m = jnp.maximum(jnp.maximum(l1, l2), l3)
        e1, e2, e3 = jnp.exp(l1 - m), jnp.exp(l2 - m), jnp.exp(l3 - m)
        ob = (e1 * o1_ref[:, cs].astype(F32) + e2 * so4_ref[c] + e3 * so16_ref[c]) / (e1 + e2 + e3)
        ob_ref[:, cs] = ob.astype(BF16)
    y_ref[...] = x_ref[...] + _dot(oa_ref[...], wa_ref[...]) + _dot(ob_ref[...], wb_ref[...])


def _out_even(x2d, oa, passes, wa, wb):
    t = x2d.shape[0]
    tm = ROW_TILE
    row = lambda width: pl.BlockSpec((tm, width), lambda i: (i, 0))
    res = lambda d: pl.BlockSpec((d, tm // d, QK_DIL), lambda i: (0, i, 0))
    (o1, l1), (o4, l4), (o16, l16) = passes
    d4, d16 = DIL_PATTERNS[1][1], DIL_PATTERNS[2][1]
    sc = pltpu.VMEM((QK_DIL // LANES, tm, LANES), F32)
    tmp = pltpu.VMEM((DIL_STRIDE, QK_DIL // LANES, tm // DIL_STRIDE, LANES), F32)
    return pl.pallas_call(
        _out_even_kernel,
        grid=(t // tm,),
        in_specs=[row(D_MODEL), row(GLA_V), row(QK_DIL), row(QK_DIL), res(d4), res(d4), res(d16), res(d16),
                  _resident((GLA_V, D_MODEL)), _resident((QK_DIL, D_MODEL))],
        out_specs=row(D_MODEL),
        out_shape=jax.ShapeDtypeStruct((t, D_MODEL), F32),
        scratch_shapes=[sc, sc, sc, sc, tmp, tmp, pltpu.VMEM((tm, QK_DIL), BF16)],
        compiler_params=_params(("parallel",)),
        name="out_even",
    )(x2d, oa, o1.reshape(t, QK_DIL), l1.reshape(t, QK_DIL), o4, l4, o16, l16, wa, wb)


def _ffn_kernel(x_ref, g_ref, wg_ref, wu_ref, wd_ref, y_ref, xn_ref, h_ref):
    x = x_ref[...]
    xn_ref[...] = _rms(x, g_ref[...]).astype(BF16)
    for ci in range(FFN_DENSE // FFN_CHUNK):
        cs = slice(ci * FFN_CHUNK, (ci + 1) * FFN_CHUNK)
        xn = xn_ref[...]
        h_ref[:, cs] = (_silu(_dot(xn, wg_ref[:, cs])) * _dot(xn, wu_ref[:, cs])).astype(BF16)
    y_ref[...] = x + _dot(h_ref[...], wd_ref[...])


def _ffn(x2d, gain, wg, wu, wd):
    t = x2d.shape[0]
    tm = ROW_TILE
    row = pl.BlockSpec((tm, D_MODEL), lambda i: (i, 0))
    return pl.pallas_call(
        _ffn_kernel,
        grid=(t // tm,),
        in_specs=[row, _resident((1, D_MODEL)), _resident((D_MODEL, FFN_DENSE)),
                  _resident((D_MODEL, FFN_DENSE)), _resident((FFN_DENSE, D_MODEL))],
        out_specs=row,
        out_shape=jax.ShapeDtypeStruct((t, D_MODEL), F32),
        scratch_shapes=[pltpu.VMEM((tm, D_MODEL), BF16), pltpu.VMEM((tm, FFN_DENSE), BF16)],
        compiler_params=_params(("parallel",)),
        name="ffn_dense",
    )(x2d, gain, wg, wu, wd)


def _in_odd_kernel(x_ref, g_ref, w_ref, h_ref):
    xn = _rms(x_ref[...], g_ref[...]).astype(BF16)
    width = RET_HEADS * RET_DK
    for ci in range(ODD_W // width):
        cs = slice(ci * width, (ci + 1) * width)
        acc = _dot(xn, w_ref[:, cs])
        if ci == 1:
            acc = acc * RET_DK ** -0.5
        if ci * width >= ODD_W - RET_HEADS * RET_DV:
            acc = _silu(acc)
        h_ref[:, cs] = acc.astype(BF16)


def _in_odd(x2d, gain, w):
    t = x2d.shape[0]
    tm = ROW_TILE
    return pl.pallas_call(
        _in_odd_kernel,
        grid=(t // tm,),
        in_specs=[pl.BlockSpec((tm, D_MODEL), lambda i: (i, 0)), _resident((1, D_MODEL)),
                  _resident((D_MODEL, ODD_W))],
        out_specs=pl.BlockSpec((tm, ODD_W), lambda i: (i, 0)),
        out_shape=jax.ShapeDtypeStruct((t, ODD_W), BF16),
        compiler_params=_params(("parallel",)),
        name="in_odd",
    )(x2d, gain, w)


def _ret_kernel(h_ref, dec_ref, qd_ref, kd_ref, cd_ref, gn_ref, o_ref, st_ref):
    @pl.when(pl.program_id(1) == 0)
    def _():
        st_ref[...] = jnp.zeros_like(st_ref)

    nk = RET_HEADS * RET_DK
    for h in range(RET_HEADS):
        ks = slice(h * RET_DK, (h + 1) * RET_DK)
        vs = slice(h * RET_DV, (h + 1) * RET_DV)
        q = h_ref[:, ks]
        k = h_ref[:, nk + h * RET_DK:nk + (h + 1) * RET_DK]
        v = h_ref[:, 2 * nk + h * RET_DV:2 * nk + (h + 1) * RET_DV]
        g = h_ref[:, 2 * nk + (RET_HEADS + h) * RET_DV:2 * nk + (RET_HEADS + h + 1) * RET_DV]
        att = (_dot_nt(q, k) * dec_ref[h]).astype(BF16)
        q_dec = (q.astype(F32) * qd_ref[h]).astype(BF16)
        k_dec = (k.astype(F32) * kd_ref[h]).astype(BF16)
        st = st_ref[h]
        o = _dot(att, v) + _dot(q_dec, st.astype(BF16))
        st_ref[h] = st * cd_ref[h] + _dot_tn(k_dec, v)
        mu = jnp.mean(o, axis=-1, keepdims=True)
        oc = o - mu
        var = jnp.mean(oc * oc, axis=-1, keepdims=True)
        o_ref[:, vs] = (oc * lax.rsqrt(var + GN_EPS) * gn_ref[:, vs] * g.astype(F32)).astype(BF16)


def _retention_tables():
    h, c = RET_HEADS, RET_CHUNK
    log_gamma = np.log1p(-np.exp2(-5.0 - np.arange(h, dtype=np.float64)))
    pos = np.arange(c, dtype=np.float64)
    rel = pos[:, None] - pos[None, :]
    decay = np.where(rel >= 0, np.exp(log_gamma[:, None, None] * np.maximum(rel, 0.0)), 0.0)
    q_dec = np.exp(log_gamma[:, None] * (pos + 1.0))[:, :, None]
    k_dec = np.exp(log_gamma[:, None] * (c - 1.0 - pos))[:, :, None]
    chunk_decay = np.broadcast_to(np.exp(log_gamma * c)[:, None, None], (h, 1, RET_DV))
    f = lambda a: jnp.asarray(np.ascontiguousarray(a), F32)
    return f(decay), f(q_dec), f(k_dec), f(chunk_decay)


def _retention(hcat, gn_w, bsz, seq):
    t = hcat.shape[0]
    c = RET_CHUNK
    ns = seq // c
    decay, q_dec, k_dec, chunk_decay = _retention_tables()
    return pl.pallas_call(
        _ret_kernel,
        grid=(bsz, ns),
        in_specs=[pl.BlockSpec((c, ODD_W), lambda b, s: (b * ns + s, 0)),
                  _resident((RET_HEADS, c, c)), _resident((RET_HEADS, c, 1)),
                  _resident((RET_HEADS, c, 1)), _resident((RET_HEADS, 1, RET_DV)),
                  _resident((1, RET_HEADS * RET_DV))],
        out_specs=pl.BlockSpec((c, RET_HEADS * RET_DV), lambda b, s: (b * ns + s, 0)),
        out_shape=jax.ShapeDtypeStruct((t, RET_HEADS * RET_DV), BF16),
        scratch_shapes=[pltpu.VMEM((RET_HEADS, RET_DK, RET_DV), F32)],
        compiler_params=_params(("parallel", "arbitrary")),
        name="retention",
    )(hcat, decay, q_dec, k_dec, chunk_decay, gn_w)


def _out_odd_kernel(x_ref, oc_ref, w_ref, g_ref, wr_ref, tri_ref,
                    y_ref, xn_ref, meta_ref, gate_ref, cnt_ref, run_ref):
    @pl.when(pl.program_id(0) == 0)
    def _():
        run_ref[...] = jnp.zeros_like(run_ref)

    y = x_ref[...] + _dot(oc_ref[...], w_ref[...])
    y_ref[...] = y
    xn = _rms(y, g_ref[...])
    for s in range(ROW_CHUNKS):
        xn_ref[_chunk_rows(s, ROW_TILE), :] = xn[:, s * LANES:(s + 1) * LANES]

    hi, lo = _split(xn)
    both = _dot(hi, wr_ref[...])
    logits = both[:, :LANES] + both[:, LANES:] + _dot(lo, wr_ref[:, :LANES])
    lane = lax.broadcasted_iota(jnp.int32, logits.shape, 1).astype(F32)
    logits = jnp.where(lane < N_EXPERTS, logits, -jnp.inf)
    t1 = jnp.max(logits, axis=-1, keepdims=True)
    i1 = jnp.min(jnp.where(logits == t1, lane, float(LANES)), axis=-1, keepdims=True)
    rest = jnp.where(lane == i1, -jnp.inf, logits)
    t2 = jnp.max(rest, axis=-1, keepdims=True)
    i2 = jnp.min(jnp.where(rest == t2, lane, float(LANES)), axis=-1, keepdims=True)
    e2 = jnp.exp(t2 - t1)
    den = 1.0 + e2
    gate_ref[...] = jnp.where(lane == 0.0, 1.0 / den, jnp.where(lane == 1.0, e2 / den, 0.0))

    sel = (lane == i1) | (lane == i2)
    rank = _dot(tri_ref[...], sel.astype(BF16)) + run_ref[...]
    r1 = jnp.sum(jnp.where(lane == i1, rank, 0.0), axis=-1, keepdims=True)
    r2 = jnp.sum(jnp.where(lane == i2, rank, 0.0), axis=-1, keepdims=True)
    run_ref[...] += jnp.sum(sel.astype(F32), axis=0, keepdims=True)
    cnt_ref[...] = run_ref[...]
    meta = jnp.where(lane == 0.0, i1, jnp.where(lane == 1.0, i2,
                     jnp.where(lane == 2.0, r1, jnp.where(lane == 3.0, r2, 0.0))))
    meta_ref[...] = meta.T[:SUBLANES, :]


def _out_odd(x2d, oc, w, gain, wr):
    t = x2d.shape[0]
    tm = ROW_TILE
    tri = jnp.asarray(np.tril(np.ones((tm, tm), np.float32), -1), BF16)
    row = lambda width: pl.BlockSpec((tm, width), lambda i: (i, 0))
    return pl.pallas_call(
        _out_odd_kernel,
        grid=(t // tm,),
        in_specs=[row(D_MODEL), row(RET_HEADS * RET_DV), _resident((RET_HEADS * RET_DV, D_MODEL)),
                  _resident((1, D_MODEL)), _resident((D_MODEL, 2 * LANES)), _resident((tm, tm))],
        out_specs=[row(D_MODEL), pl.BlockSpec((tm * ROW_CHUNKS, LANES), lambda i: (i, 0)),
                   pl.BlockSpec((SUBLANES, tm), lambda i: (0, i)), row(LANES),
                   pl.BlockSpec((1, LANES), lambda i: (0, 0))],
        out_shape=(jax.ShapeDtypeStruct((t, D_MODEL), F32),
                   jax.ShapeDtypeStruct((t * ROW_CHUNKS, LANES), F32),
                   jax.ShapeDtypeStruct((SUBLANES, t), F32),
                   jax.ShapeDtypeStruct((t, LANES), F32),
                   jax.ShapeDtypeStruct((1, LANES), F32)),
        scratch_shapes=[pltpu.VMEM((1, LANES), F32)],
        compiler_params=_params(("arbitrary",)),
        name="out_odd_router",
    )(x2d, oc, w, gain, wr, tri)


def _chunk_rows(s, n_rows):
    return pl.ds(s, n_rows, stride=ROW_CHUNKS)


def _tile_row(ref, row):
    return ref.at[pl.ds(pl.multiple_of(row * ROW_CHUNKS, ROW_CHUNKS), ROW_CHUNKS), :]


def _row_copy(src_ref, src_row, dst_ref, dst_row, sem):
    return pltpu.make_async_copy(_tile_row(src_ref, src_row), _tile_row(dst_ref, dst_row), sem)


def _dispatch_kernel(zs_ref, zon_ref, dest_ref, x_ref, xs_ref, zero_ref, sem):
    n = DISPATCH_ROWS

    @pl.when(pl.program_id(0) == 0)
    def _():
        zero_ref[...] = jnp.zeros_like(zero_ref)
        blocks = [pltpu.make_async_copy(
            zero_ref, xs_ref.at[pl.ds(pl.multiple_of(zs_ref[e] * ROW_CHUNKS, ROW_CHUNKS),
                                      MOE_ROWS * ROW_CHUNKS), :], sem) for e in range(2 * N_EXPERTS)]
        for e, cp in enumerate(blocks):
            pl.when(zon_ref[e] > 0)(cp.start)
        for e, cp in enumerate(blocks):
            pl.when(zon_ref[e] > 0)(cp.wait)

    def issue(j, carry):
        _row_copy(x_ref, j, xs_ref, dest_ref[0, 0, j], sem).start(priority=0)
        _row_copy(x_ref, j, xs_ref, dest_ref[0, 0, n + j], sem).start(priority=1)
        return carry

    lax.fori_loop(0, n, issue, 0, unroll=8)

    def drain(j, carry):
        _row_copy(x_ref, 0, xs_ref, 0, sem).wait()
        return carry

    lax.fori_loop(0, 2 * n, drain, 0, unroll=8)


def _dispatch(xn3, dest_blocks, zero_starts, zero_on, n_rows):
    t = xn3.shape[0] // ROW_CHUNKS
    n = DISPATCH_ROWS
    grid_spec = pltpu.PrefetchScalarGridSpec(
        num_scalar_prefetch=2,
        grid=(t // n,),
        in_specs=[pl.BlockSpec((1, 1, 2 * n), lambda i, zs, zon: (i, 0, 0), memory_space=pltpu.SMEM),
                  pl.BlockSpec((n * ROW_CHUNKS, LANES), lambda i, zs, zon: (i, 0))],
        out_specs=pl.BlockSpec(memory_space=pl.ANY),
        scratch_shapes=[pltpu.VMEM((MOE_ROWS * ROW_CHUNKS, LANES), F32), pltpu.SemaphoreType.DMA(())],
    )
    return pl.pallas_call(
        _dispatch_kernel,
        grid_spec=grid_spec,
        out_shape=jax.ShapeDtypeStruct((n_rows * ROW_CHUNKS, LANES), F32),
        compiler_params=_params(("arbitrary",)),
        name="moe_dispatch",
    )(zero_starts, zero_on, dest_blocks, xn3)


def _moe_kernel(be_ref, nv_ref, x_ref, wg_ref, wu_ref, wd_ref, y_ref, xb_ref, h_ref, acc_ref):
    del be_ref
    b = pl.program_id(0)
    f = pl.program_id(1)
    nf = pl.num_programs(1)

    valid = b < nv_ref[0]

    @pl.when(f == 0)
    def _():
        acc_ref[...] = jnp.zeros_like(acc_ref)

    @pl.when(valid & (f == 0))
    def _():
        for s in range(ROW_CHUNKS):
            xb_ref[:, s * LANES:(s + 1) * LANES] = x_ref[_chunk_rows(s, MOE_ROWS), :].astype(BF16)

    @pl.when(valid)
    def _():
        for ci in range(MOE_FCHUNK // MOE_INNER):
            cs = slice(ci * MOE_INNER, (ci + 1) * MOE_INNER)
            x = xb_ref[...]
            h_ref[:, cs] = (_silu(_dot(x, wg_ref[0, :, cs])) * _dot(x, wu_ref[0, :, cs])).astype(BF16)
        acc_ref[...] += _dot(h_ref[...], wd_ref[0])

    @pl.when(f == nf - 1)
    def _():
        for s in range(ROW_CHUNKS):
            y_ref[_chunk_rows(s, MOE_ROWS), :] = acc_ref[:, s * LANES:(s + 1) * LANES]


def _moe_experts(xs, block_expert, n_valid, wg, wu, wd):
    n_rows = xs.shape[0] // ROW_CHUNKS
    bm, tf = MOE_ROWS, MOE_FCHUNK
    nf = FFN_EXPERT // tf
    fidx = lambda b, f, nv: jnp.where(b < nv[0], f, nf - 1)
    rows = pl.BlockSpec((bm * ROW_CHUNKS, LANES), lambda b, f, be, nv: (b, 0))
    grid_spec = pltpu.PrefetchScalarGridSpec(
        num_scalar_prefetch=2,
        grid=(n_rows // bm, nf),
        in_specs=[rows,
                  pl.BlockSpec((1, D_MODEL, tf), lambda b, f, be, nv: (be[b], 0, fidx(b, f, nv))),
                  pl.BlockSpec((1, D_MODEL, tf), lambda b, f, be, nv: (be[b], 0, fidx(b, f, nv))),
                  pl.BlockSpec((1, tf, D_MODEL), lambda b, f, be, nv: (be[b], fidx(b, f, nv), 0))],
        out_specs=rows,
        scratch_shapes=[pltpu.VMEM((bm, D_MODEL), BF16), pltpu.VMEM((bm, tf), BF16),
                        pltpu.VMEM((bm, D_MODEL), F32)],
    )
    return pl.pallas_call(
        _moe_kernel,
        grid_spec=grid_spec,
        out_shape=jax.ShapeDtypeStruct((n_rows * ROW_CHUNKS, LANES), F32),
        compiler_params=_params(("arbitrary", "arbitrary")),
        name="moe_experts",
    )(block_expert, n_valid, xs, wg, wu, wd)


def _combine_kernel(dcur_ref, dnext_ref, x_ref, gate_ref, y_ref, o_ref, buf_ref, sem):
    n = GATHER_ROWS
    i = pl.program_id(0)
    slot = i % 2

    def issue(dref, sl):
        def body(j, carry):
            _row_copy(y_ref, dref[0, 0, j], buf_ref.at[sl], j, sem.at[sl]).start(priority=0)
            _row_copy(y_ref, dref[0, 0, n + j], buf_ref.at[sl], n + j, sem.at[sl]).start(priority=1)
            return carry
        lax.fori_loop(0, n, body, 0, unroll=8)

    @pl.when(i == 0)
    def _():
        issue(dcur_ref, 0)

    @pl.when(i + 1 < pl.num_programs(0))
    def _():
        issue(dnext_ref, 1 - slot)

    def drain(j, carry):
        _row_copy(y_ref, 0, buf_ref.at[slot], 0, sem.at[slot]).wait()
        return carry

    lax.fori_loop(0, 2 * n, drain, 0, unroll=8)

    g1 = gate_ref[:, 0:1]
    g2 = gate_ref[:, 1:2]
    for s in range(ROW_CHUNKS):
        cs = slice(s * LANES, (s + 1) * LANES)
        o_ref[:, cs] = (x_ref[:, cs] + g1 * buf_ref[slot, _chunk_rows(s, n), :]
                        + g2 * buf_ref[slot, _chunk_rows(n * ROW_CHUNKS + s, n), :])


def _combine(x3, gates, ys, dest_blocks):
    t = x3.shape[0]
    n = GATHER_ROWS
    nt = t // n
    idx = lambda f: pl.BlockSpec((1, 1, 2 * n), f, memory_space=pltpu.SMEM)
    return pl.pallas_call(
        _combine_kernel,
        grid=(nt,),
        in_specs=[idx(lambda i: (i, 0, 0)), idx(lambda i: (jnp.minimum(i + 1, nt - 1), 0, 0)),
                  pl.BlockSpec((n, D_MODEL), lambda i: (i, 0)),
                  pl.BlockSpec((n, LANES), lambda i: (i, 0)),
                  pl.BlockSpec(memory_space=pl.ANY)],
        out_specs=pl.BlockSpec((n, D_MODEL), lambda i: (i, 0)),
        out_shape=jax.ShapeDtypeStruct((t, D_MODEL), F32),
        scratch_shapes=[pltpu.VMEM((2, 2 * n * ROW_CHUNKS, LANES), F32),
                        pltpu.SemaphoreType.DMA((2,))],
        compiler_params=_params(("arbitrary",)),
        name="moe_combine",
    )(dest_blocks, dest_blocks, x3, gates, ys)


def _even_weights(w_in, q_norm, k_norm):
    a = 2 * GLA_QK + 2 * GLA_V
    pad = jnp.zeros((D_MODEL, LANES - GLA_RANK), w_in.dtype)
    w = jnp.concatenate([w_in[:, :a], w_in[:, a + GLA_RANK:], w_in[:, a:a + GLA_RANK], pad], axis=1)
    qn = (jnp.tile(q_norm, DIL_HEADS) * DIL_HD ** -0.5).reshape(1, QK_DIL)
    kn = jnp.tile(k_norm, DIL_HEADS).reshape(1, QK_DIL)
    grp = np.zeros((QK_DIL, LANES), np.float32)
    grp[np.arange(QK_DIL), np.arange(QK_DIL) // DIL_HD] = 1.0
    return (w.astype(BF16), qn.astype(F32), kn.astype(F32),
            jnp.asarray(grp, BF16), jnp.asarray(grp.T.copy(), BF16))


def kernel(x, mix_norm_even, w_in_even, w_gla_gate_up, b_gla_gate, gla_out_norm, q_norm, k_norm,
           w_out_even, ffn_norm_even, w_ffn_gate, w_ffn_up, w_ffn_down, mix_norm_odd, w_in_odd,
           ret_group_norm, w_out_odd, ffn_norm_odd, w_router, w_exp_gate, w_exp_up, w_exp_down):
    bsz, seq, d = x.shape
    t = bsz * seq
    x2d = x.reshape(t, d)
    row_vec = lambda v: v.reshape(1, -1).astype(F32)

    w_in, qn, kn, grp, grpt = _even_weights(w_in_even[0], q_norm[0], k_norm[0])
    (qk_a, v_a, r_a, lr, q1, k1, v1, q4, k4, v4, q16, k16, v16) = _in_even(
        x2d, row_vec(mix_norm_even[0]), w_in, qn, kn, grp, grpt)
    wgu = jnp.zeros((LANES, GLA_QK), F32).at[:GLA_RANK].set(w_gla_gate_up[0]).astype(BF16)
    o_a = _gla(qk_a, v_a, r_a, lr, wgu, row_vec(b_gla_gate[0]), row_vec(gla_out_norm[0]), bsz, seq)
    seqs = lambda a, dil: a.reshape(dil * bsz, seq // dil, QK_DIL)
    passes = [[o.reshape(dil, t // dil, QK_DIL) for o in
               _dilated_pass(seqs(q, dil), seqs(k, dil), seqs(v, dil), bsz, seq, dil)]
              for (_, dil), (q, k, v) in zip(DIL_PATTERNS, ((q1, k1, v1), (q4, k4, v4), (q16, k16, v16)))]
    w_out = w_out_even[0].astype(BF16)
    x2d = _out_even(x2d, o_a, passes, w_out[:GLA_V], w_out[GLA_V:])
    x2d = _ffn(x2d, row_vec(ffn_norm_even[0]),
               w_ffn_gate[0].astype(BF16), w_ffn_up[0].astype(BF16), w_ffn_down[0].astype(BF16))

    hcat = _in_odd(x2d, row_vec(mix_norm_odd[0]), w_in_odd[0].astype(BF16))
    o_c = _retention(hcat, row_vec(ret_group_norm[0]), bsz, seq)
    wr = jnp.zeros((D_MODEL, LANES), F32).at[:, :N_EXPERTS].set(w_router[0])
    wr_hi = wr.astype(BF16)
    wr_lo = (wr - wr_hi.astype(F32)).astype(BF16)
    x3, xn3, meta, gates, counts = _out_odd(x2d, o_c, w_out_odd[0].astype(BF16), row_vec(ffn_norm_odd[0]),
                                            jnp.concatenate([wr_hi, wr_lo], axis=1))

    bm = MOE_ROWS
    n_blocks = t * 2 // bm + N_EXPERTS
    n_rows = n_blocks * bm
    cnt = counts[0, :N_EXPERTS].astype(jnp.int32)
    padded = (cnt + bm - 1) // bm * bm
    pends = jnp.cumsum(padded)
    pstarts = pends - padded
    meta = meta.astype(jnp.int32)
    dest1 = pstarts[meta[0]] + meta[2]
    dest2 = pstarts[meta[1]] + meta[3]

    def dest_blocks(n):
        return jnp.concatenate([dest1.reshape(t // n, n), dest2.reshape(t // n, n)],
                               axis=1).reshape(t // n, 1, 2 * n).astype(jnp.int32)

    block_rows = jnp.arange(n_blocks, dtype=jnp.int32) * bm
    block_expert = jnp.minimum(jnp.sum(pends[None, :] <= block_rows[:, None], axis=1),
                               N_EXPERTS - 1).astype(jnp.int32)
    n_valid = (pends[-1:] // bm).astype(jnp.int32)

    tail = pends[-1] + jnp.arange(N_EXPERTS, dtype=jnp.int32) * bm
    zero_on = jnp.concatenate([padded > 0, tail + bm <= n_rows]).astype(jnp.int32)
    zero_starts = jnp.concatenate([jnp.maximum(pends - bm, 0), jnp.minimum(tail, n_rows - bm)]).astype(jnp.int32)
    xs = _dispatch(xn3, dest_blocks(DISPATCH_ROWS), zero_starts, zero_on, n_rows)
    ys = _moe_experts(xs, block_expert, n_valid, w_exp_gate[0].astype(BF16),
                      w_exp_up[0].astype(BF16), w_exp_down[0].astype(BF16))
    out = _combine(x3, gates, ys, dest_blocks(GATHER_ROWS))
    return out.reshape(bsz, seq, d)
```

```python
import numpy as np
import jax
import jax.numpy as jnp
from jax import lax
from jax.experimental import pallas as pl
from jax.experimental.pallas import tpu as pltpu

BF16 = jnp.bfloat16
F32 = jnp.float32

D_MODEL = 1024
NORM_EPS = 1e-6
GN_EPS = 1e-5
NEG_INF = -1e30

GLA_HEADS, GLA_DK, GLA_DV, GLA_RANK, GLA_TAU, GLA_CHUNK = 4, 128, 256, 16, 16.0, 64
DIL_HEADS, DIL_HD, DIL_BLOCK = 8, 64, 128
DIL_PATTERNS = ((128, 1), (512, 4), (2048, 16))
RET_HEADS, RET_DK, RET_DV = 4, 256, 512
FFN_DENSE = 2816
N_EXPERTS, FFN_EXPERT = 8, 3584

LANES = 128
SUBLANES = 8
ROW_CHUNKS = D_MODEL // LANES
VMEM_LIMIT = 56 * 1024 * 1024

ROW_TILE = 512
GLA_STEP = 512
RET_CHUNK = 256
FFN_CHUNK = 256
DIL_GROUPS = 8
MOE_ROWS = 512
MOE_FCHUNK = 1792
MOE_INNER = 256
GATHER_ROWS = 512
DISPATCH_ROWS = 1024

QK_DIL = DIL_HEADS * DIL_HD
GLA_QK = GLA_HEADS * GLA_DK
GLA_V = GLA_HEADS * GLA_DV
EVEN_W = 2 * GLA_QK + 2 * GLA_V + 3 * QK_DIL + LANES
ODD_W = 2 * RET_HEADS * RET_DK + 2 * RET_HEADS * RET_DV


def _dot(a, b):
    return jnp.dot(a, b, preferred_element_type=F32)


def _dot_nt(a, b):
    return lax.dot_general(a, b, (((1,), (1,)), ((), ())), preferred_element_type=F32)


def _dot_tn(a, b):
    return lax.dot_general(a, b, (((0,), (0,)), ((), ())), preferred_element_type=F32)


def _split(x):
    hi = x.astype(BF16)
    lo = (x - hi.astype(F32)).astype(BF16)
    return hi, lo


def _rms(x, gain, eps=NORM_EPS):
    ms = jnp.mean(x * x, axis=-1, keepdims=True)
    return x * lax.rsqrt(ms + eps) * gain


def _silu(x):
    return x / (1.0 + jnp.exp(-x))


def _log_sigmoid(z):
    return jnp.minimum(z, 0.0) - jnp.log1p(jnp.exp(-jnp.abs(z)))


def _params(sem):
    return pltpu.CompilerParams(dimension_semantics=sem, vmem_limit_bytes=VMEM_LIMIT)


def _resident(shape):
    nd = len(shape)
    return pl.BlockSpec(shape, lambda *_: (0,) * nd, pipeline_mode=pl.Buffered(1))


def _head_inv_rms(acc, grp_ref, grpt_ref):
    ss = _dot((acc * acc).astype(BF16), grp_ref[...])
    inv = lax.rsqrt(ss * (1.0 / DIL_HD) + NORM_EPS)
    ih, il = _split(inv)
    return _dot(ih, grpt_ref[...]) + _dot(il, grpt_ref[...])


DIL_STRIDE = 4
assert [d for _, d in DIL_PATTERNS] == [1, DIL_STRIDE, DIL_STRIDE * DIL_STRIDE]


def _store_by_residue(val, nat_ref, r4_ref, r16_ref, sc_ref, sc4_ref):
    nat_ref[...] = val.astype(BF16)
    nc = QK_DIL // LANES
    n4 = ROW_TILE // DIL_STRIDE
    n16 = n4 // DIL_STRIDE
    for c in range(nc):
        sc_ref[c] = val[:, c * LANES:(c + 1) * LANES]
    for b in range(DIL_STRIDE):
        for c in range(nc):
            y = sc_ref[c, pl.ds(b, n4, stride=DIL_STRIDE), :]
            r4_ref[b, :, c * LANES:(c + 1) * LANES] = y.astype(BF16)
            sc4_ref[b, c] = y
    for b in range(DIL_STRIDE):
        for a in range(DIL_STRIDE):
            for c in range(nc):
                r16_ref[DIL_STRIDE * a + b, :, c * LANES:(c + 1) * LANES] = (
                    sc4_ref[b, c, pl.ds(a, n16, stride=DIL_STRIDE), :].astype(BF16))


def _in_even_kernel(x_ref, g_ref, w_ref, qn_ref, kn_ref, grp_ref, grpt_ref,
                    qk_ref, v_ref, r_ref, lr_ref, q1_ref, k1_ref, v1_ref,
                    q4_ref, k4_ref, v4_ref, q16_ref, k16_ref, v16_ref, sc_ref, sc4_ref):
    xn = _rms(x_ref[...], g_ref[...]).astype(BF16)
    c = 2 * GLA_QK + 2 * GLA_V
    qb = _dot(xn, w_ref[:, c:c + QK_DIL])
    qb = qb * _head_inv_rms(qb, grp_ref, grpt_ref) * qn_ref[...]
    _store_by_residue(qb, q1_ref, q4_ref, q16_ref, sc_ref.at[0], sc4_ref.at[0])
    c += QK_DIL
    kb = _dot(xn, w_ref[:, c:c + QK_DIL])
    kb = kb * _head_inv_rms(kb, grp_ref, grpt_ref) * kn_ref[...]
    _store_by_residue(kb, k1_ref, k4_ref, k16_ref, sc_ref.at[1], sc4_ref.at[1])
    c += QK_DIL
    _store_by_residue(_dot(xn, w_ref[:, c:c + QK_DIL]), v1_ref, v4_ref, v16_ref, sc_ref.at[2], sc4_ref.at[2])
    c += QK_DIL
    lr_ref[...] = _dot(xn, w_ref[:, c:c + LANES])
    c = 0
    for ref, width in ((qk_ref, 2 * GLA_QK), (v_ref, GLA_V)):
        ref[...] = _dot(xn, w_ref[:, c:c + width]).astype(BF16)
        c += width
    r_ref[...] = _silu(_dot(xn, w_ref[:, c:c + GLA_V])).astype(BF16)


def _in_even(x2d, gain, w, qn, kn, grp, grpt):
    t = x2d.shape[0]
    tm = ROW_TILE
    row = lambda width: pl.BlockSpec((tm, width), lambda i: (i, 0))
    res_spec = lambda d: pl.BlockSpec((d, tm // d, QK_DIL), lambda i: (0, i, 0))
    res_shape = lambda d: jax.ShapeDtypeStruct((d, t // d, QK_DIL), BF16)
    dils = [d for _, d in DIL_PATTERNS[1:]]
    out_shape = (
        jax.ShapeDtypeStruct((t, 2 * GLA_QK), BF16),
        jax.ShapeDtypeStruct((t, GLA_V), BF16),
        jax.ShapeDtypeStruct((t, GLA_V), BF16),
        jax.ShapeDtypeStruct((t, LANES), F32),
    ) + (jax.ShapeDtypeStruct((t, QK_DIL), BF16),) * 3 + tuple(res_shape(d) for d in dils for _ in range(3))
    return pl.pallas_call(
        _in_even_kernel,
        grid=(t // tm,),
        in_specs=[row(D_MODEL), _resident((1, D_MODEL)), _resident((D_MODEL, EVEN_W)),
                  _resident((1, QK_DIL)), _resident((1, QK_DIL)),
                  _resident((QK_DIL, LANES)), _resident((LANES, QK_DIL))],
        out_specs=[row(2 * GLA_QK), row(GLA_V), row(GLA_V), row(LANES)] + [row(QK_DIL)] * 3
                  + [res_spec(d) for d in dils for _ in range(3)],
        out_shape=out_shape,
        scratch_shapes=[pltpu.VMEM((3, QK_DIL // LANES, tm, LANES), F32),
                        pltpu.VMEM((3, DIL_STRIDE, QK_DIL // LANES, tm // DIL_STRIDE, LANES), F32)],
        compiler_params=_params(("parallel",)),
        name="in_even",
    )(x2d, gain, w, qn, kn, grp, grpt)


def _gla_kernel(qk_ref, v_ref, r_ref, lr_ref, wgu_ref, bg_ref, on_ref, tri_ref, o_ref, st_ref, oi_ref):
    c = GLA_CHUNK
    chunks = [slice(ci * c, (ci + 1) * c) for ci in range(GLA_STEP // c)]
    heads = [(slice(h * GLA_DK, (h + 1) * GLA_DK), slice(h * GLA_DV, (h + 1) * GLA_DV))
             for h in range(GLA_HEADS)]

    @pl.when(pl.program_id(1) == 0)
    def _():
        st_ref[...] = jnp.zeros_like(st_ref)

    z = _dot(lr_ref[...].astype(BF16), wgu_ref[...]) + bg_ref[...]
    log_a = _log_sigmoid(z) * (1.0 / GLA_TAU)
    hi, lo = _split(log_a)
    cum = _dot(tri_ref[...], hi) + _dot(tri_ref[...], lo)
    q = qk_ref[:, 0:GLA_QK].astype(F32) * GLA_DK ** -0.5
    k = qk_ref[:, GLA_QK:2 * GLA_QK].astype(F32)
    q_dec = (q * jnp.exp(cum)).astype(BF16)
    k_inv = (k * jnp.exp(-cum)).astype(BF16)
    totals = [cum[rs.stop - 1:rs.stop, :] for rs in chunks]
    k_end = [(k[rs] * jnp.exp(tot - cum[rs])).astype(BF16) for rs, tot in zip(chunks, totals)]
    e_tot = [jnp.exp(tot) for tot in totals]

    rows = lax.broadcasted_iota(jnp.int32, (c, c), 0)
    cols = lax.broadcasted_iota(jnp.int32, (c, c), 1)
    causal = rows >= cols
    for rs in chunks:
        for ks, vs in heads:
            att = jnp.where(causal, _dot_nt(q_dec[rs, ks], k_inv[rs, ks]), 0.0).astype(BF16)
            oi_ref[rs, vs] = _dot(att, v_ref[rs, vs])

    for ci, rs in enumerate(chunks):
        for h, (ks, vs) in enumerate(heads):
            st = st_ref[h]
            oi_ref[rs, vs] += _dot_nt(q_dec[rs, ks], st.astype(BF16))
            st_ref[h] = st * e_tot[ci][:, ks] + _dot_tn(v_ref[rs, vs], k_end[ci][:, ks])

    for _, vs in heads:
        o_ref[:, vs] = (_rms(oi_ref[:, vs], on_ref[...]) * r_ref[:, vs].astype(F32)).astype(BF16)


def _gla(qk, v, r, lr, wgu, bg, on, bsz, seq):
    t = qk.shape[0]
    ns = seq // GLA_STEP
    row = lambda width: pl.BlockSpec((GLA_STEP, width), lambda b, s: (b * ns + s, 0))
    pos = np.arange(GLA_STEP)
    same_chunk = pos[:, None] // GLA_CHUNK == pos[None, :] // GLA_CHUNK
    tri = jnp.asarray(same_chunk & (pos[:, None] >= pos[None, :]), BF16)
    return pl.pallas_call(
        _gla_kernel,
        grid=(bsz, ns),
        in_specs=[row(2 * GLA_QK), row(GLA_V), row(GLA_V), row(LANES),
                  _resident((LANES, GLA_QK)), _resident((1, GLA_QK)), _resident((1, GLA_DV)),
                  _resident((GLA_STEP, GLA_STEP))],
        out_specs=row(GLA_V),
        out_shape=jax.ShapeDtypeStruct((t, GLA_V), BF16),
        scratch_shapes=[pltpu.VMEM((GLA_HEADS, GLA_DV, GLA_DK), F32), pltpu.VMEM((GLA_STEP, GLA_V), F32)],
        compiler_params=_params(("parallel", "arbitrary")),
        name="gla",
    )(qk, v, r, lr, wgu, bg, on, tri)


def _dil_kernel(q_ref, k_ref, v_ref, e_ref, b_ref, a_ref, o_ref, lse_ref, kp_ref, vp_ref):
    blk = DIL_BLOCK
    n = pl.program_id(1)

    @pl.when(n == 0)
    def _():
        kp_ref[...] = jnp.zeros_like(kp_ref)
        vp_ref[...] = jnp.zeros_like(vp_ref)

    qi = lax.broadcasted_iota(jnp.int32, (2 * blk, 2 * blk), 0) & (blk - 1)
    kj = lax.broadcasted_iota(jnp.int32, (2 * blk, 2 * blk), 1)
    valid = ((kj < blk) & (kj >= qi) & (n > 0)) | ((kj >= blk) & (kj - blk <= qi))
    first = lax.broadcasted_iota(jnp.int32, (blk, LANES), 1) < DIL_HD
    ones = jnp.ones((2 * blk, LANES), BF16)
    for g in range(q_ref.shape[0]):
        for c in range(QK_DIL // LANES):
            cs = slice(c * LANES, (c + 1) * LANES)
            q = q_ref[g, :, cs]
            zero = jnp.zeros_like(q)
            q2 = jnp.concatenate([jnp.where(first, q, zero), jnp.where(first, zero, q)], axis=0)
            q_aug = jnp.concatenate([q2, e_ref[...]], axis=1)
            k_aug = jnp.concatenate([jnp.concatenate([kp_ref[g, :, cs], k_ref[g, :, cs]], axis=0), b_ref[c]], axis=1)
            v_aug = jnp.concatenate([jnp.concatenate([vp_ref[g, :, cs], v_ref[g, :, cs]], axis=0), ones], axis=1)
            s = jnp.where(valid, _dot_nt(q_aug, k_aug), NEG_INF)
            m = jnp.max(s, axis=-1, keepdims=True)
            p = jnp.exp(s - m).astype(BF16)
            ov = _dot(p, v_aug)
            den = ov[:, LANES:]
            o2 = ov[:, :LANES] / den
            l2 = m + jnp.log(den) + a_ref[c]
            o_ref[g, :, cs] = jnp.where(first, o2[:blk], o2[blk:]).astype(BF16)
            lse_ref[g, :, cs] = jnp.where(first, l2[:blk], l2[blk:])
    kp_ref[...] = k_ref[...]
    vp_ref[...] = v_ref[...]


def _dilated_tables(dilation):
    blk = DIL_BLOCK
    slopes = np.exp2(-8.0 * np.arange(1, DIL_HEADS + 1) / DIL_HEADS)
    npair = QK_DIL // LANES
    e = np.zeros((2 * blk, LANES), np.float32)
    e[:blk, 0] = 1.0
    e[blk:, 1] = 1.0
    b = np.zeros((npair, 2 * blk, LANES), np.float32)
    a = np.zeros((npair, 2 * blk, LANES), np.float32)
    key_off = np.arange(2 * blk, dtype=np.float64) - blk
    qpos = np.arange(blk, dtype=np.float64)
    for c in range(npair):
        for half in range(2):
            sl = slopes[2 * c + half] * dilation
            b[c, :, half] = sl * key_off
            a[c, half * blk:(half + 1) * blk, :] = (-sl * qpos)[:, None]
    return jnp.asarray(e, BF16), jnp.asarray(b, BF16), jnp.asarray(a, F32)


def _dilated_pass(q, k, v, bsz, seq, dilation):
    seg = seq // dilation
    nb = seg // DIL_BLOCK
    ng = dilation * bsz
    groups = min(DIL_GROUPS, ng)
    assert ng % groups == 0 and seg % DIL_BLOCK == 0
    e, b, a = _dilated_tables(dilation)
    blk = pl.BlockSpec((groups, DIL_BLOCK, QK_DIL), lambda gi, n: (gi, n, 0))
    scratch = pltpu.VMEM((groups, DIL_BLOCK, QK_DIL), BF16)
    return pl.pallas_call(
        _dil_kernel,
        grid=(ng // groups, nb),
        in_specs=[blk, blk, blk, _resident(e.shape), _resident(b.shape), _resident(a.shape)],
        out_specs=[blk, blk],
        out_shape=(jax.ShapeDtypeStruct((ng, seg, QK_DIL), BF16), jax.ShapeDtypeStruct((ng, seg, QK_DIL), F32)),
        scratch_shapes=[scratch, scratch],
        compiler_params=_params(("parallel", "arbitrary")),
        name=f"dilated_d{dilation}",
    )(q, k, v, e, b, a)


def _to_natural(r4_ref, r16_ref, sc4_ref, sc16_ref, tmp_ref):
    nc = QK_DIL // LANES
    n4 = ROW_TILE // DIL_STRIDE
    n16 = n4 // DIL_STRIDE
    for b in range(DIL_STRIDE):
        for c in range(nc):
            cs = slice(c * LANES, (c + 1) * LANES)
            sc4_ref[c, pl.ds(b, n4, stride=DIL_STRIDE), :] = r4_ref[b, :, cs].astype(F32)
            for a in range(DIL_STRIDE):
                tmp_ref[b, c, pl.ds(a, n16, stride=DIL_STRIDE), :] = r16_ref[DIL_STRIDE * a + b, :, cs].astype(F32)
    for b in range(DIL_STRIDE):
        for c in range(nc):
            sc16_ref[c, pl.ds(b, n4, stride=DIL_STRIDE), :] = tmp_ref[b, c]


def _out_even_kernel(x_ref, oa_ref, o1_ref, l1_ref, o4_ref, l4_ref, o16_ref, l16_ref,
                     wa_ref, wb_ref, g_ref, wg_ref, wu_ref, wd_ref, y_ref,
                     so4_ref, sl4_ref, so16_ref, sl16_ref, to_ref, tl_ref, ob_ref, xn_ref, h_ref):
    _to_natural(o4_ref, o16_ref, so4_ref, so16_ref, to_ref)
    _to_natural(l4_ref, l16_ref, sl4_ref, sl16_ref, tl_ref)
    for c in range(QK_DIL // LANES):
        cs = slice(c * LANES, (c + 1) * LANES)
        l1, l2, l3 = l1_ref[:, cs], sl4_ref[c], sl16_ref[c]
        m = jnp.maximum(jnp.maximum(l1, l2), l3)
        e1, e2, e3 = jnp.exp(l1 - m), jnp.exp(l2 - m), jnp.exp(l3 - m)
        ob = (e1 * o1_ref[:, cs].astype(F32) + e2 * so4_ref[c] + e3 * so16_ref[c]) / (e1 + e2 + e3)
        ob_ref[:, cs] = ob.astype(BF16)
    y_ref[...] = x_ref[...] + _dot(oa_ref[...], wa_ref[...]) + _dot(ob_ref[...], wb_ref[...])
    xn_ref[...] = _rms(y_ref[...], g_ref[...]).astype(BF16)
    for ci in range(FFN_DENSE // FFN_CHUNK):
        cs = slice(ci * FFN_CHUNK, (ci + 1) * FFN_CHUNK)
        xn = xn_ref[...]
        h_ref[:, cs] = (_silu(_dot(xn, wg_ref[:, cs])) * _dot(xn, wu_ref[:, cs])).astype(BF16)
    y_ref[...] += _dot(h_ref[...], wd_ref[...])


def _out_even_ffn(x2d, oa, passes, wa, wb, gain, wg, wu, wd):
    t = x2d.shape[0]
    tm = ROW_TILE
    row = lambda width: pl.BlockSpec((tm, width), lambda i: (i, 0))
    res = lambda d: pl.BlockSpec((d, tm // d, QK_DIL), lambda i: (0, i, 0))
    (o1, l1), (o4, l4), (o16, l16) = passes
    d4, d16 = DIL_PATTERNS[1][1], DIL_PATTERNS[2][1]
    sc = pltpu.VMEM((QK_DIL // LANES, tm, LANES), F32)
    tmp = pltpu.VMEM((DIL_STRIDE, QK_DIL // LANES, tm // DIL_STRIDE, LANES), F32)
    return pl.pallas_call(
        _out_even_kernel,
        grid=(t // tm,),
        in_specs=[row(D_MODEL), row(GLA_V), row(QK_DIL), row(QK_DIL), res(d4), res(d4), res(d16), res(d16),
                  _resident((GLA_V, D_MODEL)), _resident((QK_DIL, D_MODEL)), _resident((1, D_MODEL)),
                  _resident((D_MODEL, FFN_DENSE)), _resident((D_MODEL, FFN_DENSE)),
                  _resident((FFN_DENSE, D_MODEL))],
        out_specs=row(D_MODEL),
        out_shape=jax.ShapeDtypeStruct((t, D_MODEL), F32),
        scratch_shapes=[sc, sc, sc, sc, tmp, tmp, pltpu.VMEM((tm, QK_DIL), BF16),
                        pltpu.VMEM((tm, D_MODEL), BF16), pltpu.VMEM((tm, FFN_DENSE), BF16)],
        compiler_params=_params(("parallel",)),
        name="out_even_ffn",
    )(x2d, oa, o1.reshape(t, QK_DIL), l1.reshape(t, QK_DIL), o4, l4, o16, l16, wa, wb, gain, wg, wu, wd)


def _in_odd_kernel(x_ref, g_ref, w_ref, h_ref):
    xn = _rms(x_ref[...], g_ref[...]).astype(BF16)
    width = RET_HEADS * RET_DK
    for ci in range(ODD_W // width):
        cs = slice(ci * width, (ci + 1) * width)
        acc = _dot(xn, w_ref[:, cs])
        if ci == 1:
            acc = acc * RET_DK ** -0.5
        if ci * width >= ODD_W - RET_HEADS * RET_DV:
            acc = _silu(acc)
        h_ref[:, cs] = acc.astype(BF16)


def _in_odd(x2d, gain, w):
    t = x2d.shape[0]
    tm = ROW_TILE
    return pl.pallas_call(
        _in_odd_kernel,
        grid=(t // tm,),
        in_specs=[pl.BlockSpec((tm, D_MODEL), lambda i: (i, 0)), _resident((1, D_MODEL)),
                  _resident((D_MODEL, ODD_W))],
        out_specs=pl.BlockSpec((tm, ODD_W), lambda i: (i, 0)),
        out_shape=jax.ShapeDtypeStruct((t, ODD_W), BF16),
        compiler_params=_params(("parallel",)),
        name="in_odd",
    )(x2d, gain, w)


def _ret_kernel(h_ref, dec_ref, qd_ref, kd_ref, cd_ref, gn_ref, o_ref, st_ref):
    @pl.when(pl.program_id(1) == 0)
    def _():
        st_ref[...] = jnp.zeros_like(st_ref)

    nk = RET_HEADS * RET_DK
    for h in range(RET_HEADS):
        ks = slice(h * RET_DK, (h + 1) * RET_DK)
        vs = slice(h * RET_DV, (h + 1) * RET_DV)
        q = h_ref[:, ks]
        k = h_ref[:, nk + h * RET_DK:nk + (h + 1) * RET_DK]
        v = h_ref[:, 2 * nk + h * RET_DV:2 * nk + (h + 1) * RET_DV]
        g = h_ref[:, 2 * nk + (RET_HEADS + h) * RET_DV:2 * nk + (RET_HEADS + h + 1) * RET_DV]
        att = (_dot_nt(q, k) * dec_ref[h]).astype(BF16)
        q_dec = (q.astype(F32) * qd_ref[h]).astype(BF16)
        k_dec = (k.astype(F32) * kd_ref[h]).astype(BF16)
        st = st_ref[h]
        o = _dot(att, v) + _dot(q_dec, st.astype(BF16))
        st_ref[h] = st * cd_ref[h] + _dot_tn(k_dec, v)
        mu = jnp.mean(o, axis=-1, keepdims=True)
        oc = o - mu
        var = jnp.mean(oc * oc, axis=-1, keepdims=True)
        o_ref[:, vs] = (oc * lax.rsqrt(var + GN_EPS) * gn_ref[:, vs] * g.astype(F32)).astype(BF16)


def _retention_tables():
    h, c = RET_HEADS, RET_CHUNK
    log_gamma = np.log1p(-np.exp2(-5.0 - np.arange(h, dtype=np.float64)))
    pos = np.arange(c, dtype=np.float64)
    rel = pos[:, None] - pos[None, :]
    decay = np.where(rel >= 0, np.exp(log_gamma[:, None, None] * np.maximum(rel, 0.0)), 0.0)
    q_dec = np.exp(log_gamma[:, None] * (pos + 1.0))[:, :, None]
    k_dec = np.exp(log_gamma[:, None] * (c - 1.0 - pos))[:, :, None]
    chunk_decay = np.broadcast_to(np.exp(log_gamma * c)[:, None, None], (h, 1, RET_DV))
    f = lambda a: jnp.asarray(np.ascontiguousarray(a), F32)
    return f(decay), f(q_dec), f(k_dec), f(chunk_decay)


def _retention(hcat, gn_w, bsz, seq):
    t = hcat.shape[0]
    c = RET_CHUNK
    ns = seq // c
    decay, q_dec, k_dec, chunk_decay = _retention_tables()
    return pl.pallas_call(
        _ret_kernel,
        grid=(bsz, ns),
        in_specs=[pl.BlockSpec((c, ODD_W), lambda b, s: (b * ns + s, 0)),
                  _resident((RET_HEADS, c, c)), _resident((RET_HEADS, c, 1)),
                  _resident((RET_HEADS, c, 1)), _resident((RET_HEADS, 1, RET_DV)),
                  _resident((1, RET_HEADS * RET_DV))],
        out_specs=pl.BlockSpec((c, RET_HEADS * RET_DV), lambda b, s: (b * ns + s, 0)),
        out_shape=jax.ShapeDtypeStruct((t, RET_HEADS * RET_DV), BF16),
        scratch_shapes=[pltpu.VMEM((RET_HEADS, RET_DK, RET_DV), F32)],
        compiler_params=_params(("parallel", "arbitrary")),
        name="retention",
    )(hcat, decay, q_dec, k_dec, chunk_decay, gn_w)


def _out_odd_kernel(x_ref, oc_ref, w_ref, g_ref, wr_ref, tri_ref,
                    y_ref, xn_ref, meta_ref, gate_ref, cnt_ref, run_ref):
    @pl.when(pl.program_id(0) == 0)
    def _():
        run_ref[...] = jnp.zeros_like(run_ref)

    y = x_ref[...] + _dot(oc_ref[...], w_ref[...])
    y_ref[...] = y
    xn = _rms(y, g_ref[...])
    for s in range(ROW_CHUNKS):
        xn_ref[_chunk_rows(s, ROW_TILE), :] = xn[:, s * LANES:(s + 1) * LANES]

    hi, lo = _split(xn)
    both = _dot(hi, wr_ref[...])
    logits = both[:, :LANES] + both[:, LANES:] + _dot(lo, wr_ref[:, :LANES])
    lane = lax.broadcasted_iota(jnp.int32, logits.shape, 1).astype(F32)
    logits = jnp.where(lane < N_EXPERTS, logits, -jnp.inf)
    t1 = jnp.max(logits, axis=-1, keepdims=True)
    i1 = jnp.min(jnp.where(logits == t1, lane, float(LANES)), axis=-1, keepdims=True)
    rest = jnp.where(lane == i1, -jnp.inf, logits)
    t2 = jnp.max(rest, axis=-1, keepdims=True)
    i2 = jnp.min(jnp.where(rest == t2, lane, float(LANES)), axis=-1, keepdims=True)
    e2 = jnp.exp(t2 - t1)
    den = 1.0 + e2
    gate_ref[...] = jnp.where(lane == 0.0, 1.0 / den, jnp.where(lane == 1.0, e2 / den, 0.0))

    sel = (lane == i1) | (lane == i2)
    rank = _dot(tri_ref[...], sel.astype(BF16)) + run_ref[...]
    r1 = jnp.sum(jnp.where(lane == i1, rank, 0.0), axis=-1, keepdims=True)
    r2 = jnp.sum(jnp.where(lane == i2, rank, 0.0), axis=-1, keepdims=True)
    run_ref[...] += jnp.sum(sel.astype(F32), axis=0, keepdims=True)
    cnt_ref[...] = run_ref[...]
    meta = jnp.where(lane == 0.0, i1, jnp.where(lane == 1.0, i2,
                     jnp.where(lane == 2.0, r1, jnp.where(lane == 3.0, r2, 0.0))))
    meta_ref[...] = meta.T[:SUBLANES, :]


def _out_odd(x2d, oc, w, gain, wr):
    t = x2d.shape[0]
    tm = ROW_TILE
    tri = jnp.asarray(np.tril(np.ones((tm, tm), np.float32), -1), BF16)
    row = lambda width: pl.BlockSpec((tm, width), lambda i: (i, 0))
    return pl.pallas_call(
        _out_odd_kernel,
        grid=(t // tm,),
        in_specs=[row(D_MODEL), row(RET_HEADS * RET_DV), _resident((RET_HEADS * RET_DV, D_MODEL)),
                  _resident((1, D_MODEL)), _resident((D_MODEL, 2 * LANES)), _resident((tm, tm))],
        out_specs=[row(D_MODEL), pl.BlockSpec((tm * ROW_CHUNKS, LANES), lambda i: (i, 0)),
                   pl.BlockSpec((SUBLANES, tm), lambda i: (0, i)), row(LANES),
                   pl.BlockSpec((1, LANES), lambda i: (0, 0))],
        out_shape=(jax.ShapeDtypeStruct((t, D_MODEL), F32),
                   jax.ShapeDtypeStruct((t * ROW_CHUNKS, LANES), F32),
                   jax.ShapeDtypeStruct((SUBLANES, t), F32),
                   jax.ShapeDtypeStruct((t, LANES), F32),
                   jax.ShapeDtypeStruct((1, LANES), F32)),
        scratch_shapes=[pltpu.VMEM((1, LANES), F32)],
        compiler_params=_params(("arbitrary",)),
        name="out_odd_router",
    )(x2d, oc, w, gain, wr, tri)


def _chunk_rows(s, n_rows):
    return pl.ds(s, n_rows, stride=ROW_CHUNKS)


def _tile_row(ref, row):
    return ref.at[pl.ds(pl.multiple_of(row * ROW_CHUNKS, ROW_CHUNKS), ROW_CHUNKS), :]


def _row_copy(src_ref, src_row, dst_ref, dst_row, sem):
    return pltpu.make_async_copy(_tile_row(src_ref, src_row), _tile_row(dst_ref, dst_row), sem)


def _dispatch_kernel(zs_ref, zon_ref, dest_ref, x_ref, xs_ref, zero_ref, sem):
    n = DISPATCH_ROWS

    @pl.when(pl.program_id(0) == 0)
    def _():
        zero_ref[...] = jnp.zeros_like(zero_ref)
        blocks = [pltpu.make_async_copy(
            zero_ref, xs_ref.at[pl.ds(pl.multiple_of(zs_ref[e] * ROW_CHUNKS, ROW_CHUNKS),
                                      MOE_ROWS * ROW_CHUNKS), :], sem) for e in range(2 * N_EXPERTS)]
        for e, cp in enumerate(blocks):
            pl.when(zon_ref[e] > 0)(cp.start)
        for e, cp in enumerate(blocks):
            pl.when(zon_ref[e] > 0)(cp.wait)

    def issue(j, carry):
        _row_copy(x_ref, j, xs_ref, dest_ref[0, 0, j], sem).start(priority=0)
        _row_copy(x_ref, j, xs_ref, dest_ref[0, 0, n + j], sem).start(priority=1)
        return carry

    lax.fori_loop(0, n, issue, 0, unroll=8)

    def drain(j, carry):
        _row_copy(x_ref, 0, xs_ref, 0, sem).wait()
        return carry

    lax.fori_loop(0, 2 * n, drain, 0, unroll=8)


def _dispatch(xn3, dest_blocks, zero_starts, zero_on, n_rows):
    t = xn3.shape[0] // ROW_CHUNKS
    n = DISPATCH_ROWS
    grid_spec = pltpu.PrefetchScalarGridSpec(
        num_scalar_prefetch=2,
        grid=(t // n,),
        in_specs=[pl.BlockSpec((1, 1, 2 * n), lambda i, zs, zon: (i, 0, 0), memory_space=pltpu.SMEM),
                  pl.BlockSpec((n * ROW_CHUNKS, LANES), lambda i, zs, zon: (i, 0))],
        out_specs=pl.BlockSpec(memory_space=pl.ANY),
        scratch_shapes=[pltpu.VMEM((MOE_ROWS * ROW_CHUNKS, LANES), F32), pltpu.SemaphoreType.DMA(())],
    )
    return pl.pallas_call(
        _dispatch_kernel,
        grid_spec=grid_spec,
        out_shape=jax.ShapeDtypeStruct((n_rows * ROW_CHUNKS, LANES), F32),
        compiler_params=_params(("arbitrary",)),
        name="moe_dispatch",
    )(zero_starts, zero_on, dest_blocks, xn3)


def _moe_kernel(be_ref, nv_ref, x_ref, wg_ref, wu_ref, wd_ref, y_ref, xb_ref, h_ref, acc_ref):
    del be_ref
    b = pl.program_id(0)
    f = pl.program_id(1)
    nf = pl.num_programs(1)

    valid = b < nv_ref[0]

    @pl.when(f == 0)
    def _():
        acc_ref[...] = jnp.zeros_like(acc_ref)

    @pl.when(valid & (f == 0))
    def _():
        for s in range(ROW_CHUNKS):
            xb_ref[:, s * LANES:(s + 1) * LANES] = x_ref[_chunk_rows(s, MOE_ROWS), :].astype(BF16)

    @pl.when(valid)
    def _():
        for ci in range(MOE_FCHUNK // MOE_INNER):
            cs = slice(ci * MOE_INNER, (ci + 1) * MOE_INNER)
            x = xb_ref[...]
            h_ref[:, cs] = (_silu(_dot(x, wg_ref[0, :, cs])) * _dot(x, wu_ref[0, :, cs])).astype(BF16)
        acc_ref[...] += _dot(h_ref[...], wd_ref[0])

    @pl.when(f == nf - 1)
    def _():
        for s in range(ROW_CHUNKS):
            y_ref[_chunk_rows(s, MOE_ROWS), :] = acc_ref[:, s * LANES:(s + 1) * LANES]


def _moe_experts(xs, block_expert, n_valid, wg, wu, wd):
    n_rows = xs.shape[0] // ROW_CHUNKS
    bm, tf = MOE_ROWS, MOE_FCHUNK
    nf = FFN_EXPERT // tf
    fidx = lambda b, f, nv: jnp.where(b < nv[0], f, nf - 1)
    rows = pl.BlockSpec((bm * ROW_CHUNKS, LANES), lambda b, f, be, nv: (b, 0))
    grid_spec = pltpu.PrefetchScalarGridSpec(
        num_scalar_prefetch=2,
        grid=(n_rows // bm, nf),
        in_specs=[rows,
                  pl.BlockSpec((1, D_MODEL, tf), lambda b, f, be, nv: (be[b], 0, fidx(b, f, nv))),
                  pl.BlockSpec((1, D_MODEL, tf), lambda b, f, be, nv: (be[b], 0, fidx(b, f, nv))),
                  pl.BlockSpec((1, tf, D_MODEL), lambda b, f, be, nv: (be[b], fidx(b, f, nv), 0))],
        out_specs=rows,
        scratch_shapes=[pltpu.VMEM((bm, D_MODEL), BF16), pltpu.VMEM((bm, tf), BF16),
                        pltpu.VMEM((bm, D_MODEL), F32)],
    )
    return pl.pallas_call(
        _moe_kernel,
        grid_spec=grid_spec,
        out_shape=jax.ShapeDtypeStruct((n_rows * ROW_CHUNKS, LANES), F32),
        compiler_params=_params(("arbitrary", "arbitrary")),
        name="moe_experts",
    )(block_expert, n_valid, xs, wg, wu, wd)


def _combine_kernel(dcur_ref, dnext_ref, x_ref, gate_ref, y_ref, o_ref, buf_ref, sem):
    n = GATHER_ROWS
    i = pl.program_id(0)
    slot = i % 2

    def issue(dref, sl):
        def body(j, carry):
            _row_copy(y_ref, dref[0, 0, j], buf_ref.at[sl], j, sem.at[sl]).start(priority=0)
            _row_copy(y_ref, dref[0, 0, n + j], buf_ref.at[sl], n + j, sem.at[sl]).start(priority=1)
            return carry
        lax.fori_loop(0, n, body, 0, unroll=8)

    @pl.when(i == 0)
    def _():
        issue(dcur_ref, 0)

    @pl.when(i + 1 < pl.num_programs(0))
    def _():
        issue(dnext_ref, 1 - slot)

    def drain(j, carry):
        _row_copy(y_ref, 0, buf_ref.at[slot], 0, sem.at[slot]).wait()
        return carry

    lax.fori_loop(0, 2 * n, drain, 0, unroll=8)

    g1 = gate_ref[:, 0:1]
    g2 = gate_ref[:, 1:2]
    for s in range(ROW_CHUNKS):
        cs = slice(s * LANES, (s + 1) * LANES)
        o_ref[:, cs] = (x_ref[:, cs] + g1 * buf_ref[slot, _chunk_rows(s, n), :]
                        + g2 * buf_ref[slot, _chunk_rows(n * ROW_CHUNKS + s, n), :])


def _combine(x3, gates, ys, dest_blocks):
    t = x3.shape[0]
    n = GATHER_ROWS
    nt = t // n
    idx = lambda f: pl.BlockSpec((1, 1, 2 * n), f, memory_space=pltpu.SMEM)
    return pl.pallas_call(
        _combine_kernel,
        grid=(nt,),
        in_specs=[idx(lambda i: (i, 0, 0)), idx(lambda i: (jnp.minimum(i + 1, nt - 1), 0, 0)),
                  pl.BlockSpec((n, D_MODEL), lambda i: (i, 0)),
                  pl.BlockSpec((n, LANES), lambda i: (i, 0)),
                  pl.BlockSpec(memory_space=pl.ANY)],
        out_specs=pl.BlockSpec((n, D_MODEL), lambda i: (i, 0)),
        out_shape=jax.ShapeDtypeStruct((t, D_MODEL), F32),
        scratch_shapes=[pltpu.VMEM((2, 2 * n * ROW_CHUNKS, LANES), F32),
                        pltpu.SemaphoreType.DMA((2,))],
        compiler_params=_params(("arbitrary",)),
        name="moe_combine",
    )(dest_blocks, dest_blocks, x3, gates, ys)


def _even_weights(w_in, q_norm, k_norm):
    a = 2 * GLA_QK + 2 * GLA_V
    pad = jnp.zeros((D_MODEL, LANES - GLA_RANK), w_in.dtype)
    w = jnp.concatenate([w_in[:, :a], w_in[:, a + GLA_RANK:], w_in[:, a:a + GLA_RANK], pad], axis=1)
    qn = (jnp.tile(q_norm, DIL_HEADS) * DIL_HD ** -0.5).reshape(1, QK_DIL)
    kn = jnp.tile(k_norm, DIL_HEADS).reshape(1, QK_DIL)
    grp = np.zeros((QK_DIL, LANES), np.float32)
    grp[np.arange(QK_DIL), np.arange(QK_DIL) // DIL_HD] = 1.0
    return (w.astype(BF16), qn.astype(F32), kn.astype(F32),
            jnp.asarray(grp, BF16), jnp.asarray(grp.T.copy(), BF16))


def kernel(x, mix_norm_even, w_in_even, w_gla_gate_up, b_gla_gate, gla_out_norm, q_norm, k_norm,
           w_out_even, ffn_norm_even, w_ffn_gate, w_ffn_up, w_ffn_down, mix_norm_odd, w_in_odd,
           ret_group_norm, w_out_odd, ffn_norm_odd, w_router, w_exp_gate, w_exp_up, w_exp_down):
    bsz, seq, d = x.shape
    t = bsz * seq
    x2d = x.reshape(t, d)
    row_vec = lambda v: v.reshape(1, -1).astype(F32)

    w_in, qn, kn, grp, grpt = _even_weights(w_in_even[0], q_norm[0], k_norm[0])
    (qk_a, v_a, r_a, lr, q1, k1, v1, q4, k4, v4, q16, k16, v16) = _in_even(
        x2d, row_vec(mix_norm_even[0]), w_in, qn, kn, grp, grpt)
    wgu = jnp.zeros((LANES, GLA_QK), F32).at[:GLA_RANK].set(w_gla_gate_up[0]).astype(BF16)
    o_a = _gla(qk_a, v_a, r_a, lr, wgu, row_vec(b_gla_gate[0]), row_vec(gla_out_norm[0]), bsz, seq)
    seqs = lambda a, dil: a.reshape(dil * bsz, seq // dil, QK_DIL)
    passes = [[o.reshape(dil, t // dil, QK_DIL) for o in
               _dilated_pass(seqs(q, dil), seqs(k, dil), seqs(v, dil), bsz, seq, dil)]
              for (_, dil), (q, k, v) in zip(DIL_PATTERNS, ((q1, k1, v1), (q4, k4, v4), (q16, k16, v16)))]
    w_out = w_out_even[0].astype(BF16)
    x2d = _out_even_ffn(x2d, o_a, passes, w_out[:GLA_V], w_out[GLA_V:], row_vec(ffn_norm_even[0]),
                        w_ffn_gate[0].astype(BF16), w_ffn_up[0].astype(BF16), w_ffn_down[0].astype(BF16))

    hcat = _in_odd(x2d, row_vec(mix_norm_odd[0]), w_in_odd[0].astype(BF16))
    o_c = _retention(hcat, row_vec(ret_group_norm[0]), bsz, seq)
    wr = jnp.zeros((D_MODEL, LANES), F32).at[:, :N_EXPERTS].set(w_router[0])
    wr_hi = wr.astype(BF16)
    wr_lo = (wr - wr_hi.astype(F32)).astype(BF16)
    x3, xn3, meta, gates, counts = _out_odd(x2d, o_c, w_out_odd[0].astype(BF16), row_vec(ffn_norm_odd[0]),
                                            jnp.concatenate([wr_hi, wr_lo], axis=1))

    bm = MOE_ROWS
    n_blocks = t * 2 // bm + N_EXPERTS
    n_rows = n_blocks * bm
    cnt = counts[0, :N_EXPERTS].astype(jnp.int32)
    padded = (cnt + bm - 1) // bm * bm
    pends = jnp.cumsum(padded)
    pstarts = pends - padded
    meta = meta.astype(jnp.int32)
    dest1 = pstarts[meta[0]] + meta[2]
    dest2 = pstarts[meta[1]] + meta[3]

    def dest_blocks(n):
        return jnp.concatenate([dest1.reshape(t // n, n), dest2.reshape(t // n, n)],
                               axis=1).reshape(t // n, 1, 2 * n).astype(jnp.int32)

    block_rows = jnp.arange(n_blocks, dtype=jnp.int32) * bm
    block_expert = jnp.minimum(jnp.sum(pends[None, :] <= block_rows[:, None], axis=1),
                               N_EXPERTS - 1).astype(jnp.int32)
    n_valid = (pends[-1:] // bm).astype(jnp.int32)

    tail = pends[-1] + jnp.arange(N_EXPERTS, dtype=jnp.int32) * bm
    zero_on = jnp.concatenate([padded > 0, tail + bm <= n_rows]).astype(jnp.int32)
    zero_starts = jnp.concatenate([jnp.maximum(pends - bm, 0), jnp.minimum(tail, n_rows - bm)]).astype(jnp.int32)
    xs = _dispatch(xn3, dest_blocks(DISPATCH_ROWS), zero_starts, zero_on, n_rows)
    ys = _moe_experts(xs, block_expert, n_valid, w_exp_gate[0].astype(BF16),
                      w_exp_up[0].astype(BF16), w_exp_down[0].astype(BF16))
    out = _combine(x3, gates, ys, dest_blocks(GATHER_ROWS))
    return out.reshape(bsz, seq, d)
```

```python
import numpy as np
import jax
import jax.numpy as jnp
from jax import lax
from jax.experimental import pallas as pl
from jax.experimental.pallas import tpu as pltpu

BF16 = jnp.bfloat16
F32 = jnp.float32

D_MODEL = 1024
NORM_EPS = 1e-6
GN_EPS = 1e-5
NEG_INF = -1e30

GLA_HEADS, GLA_DK, GLA_DV, GLA_RANK, GLA_TAU, GLA_CHUNK = 4, 128, 256, 16, 16.0, 64
DIL_HEADS, DIL_HD, DIL_BLOCK = 8, 64, 128
DIL_PATTERNS = ((128, 1), (512, 4), (2048, 16))
RET_HEADS, RET_DK, RET_DV = 4, 256, 512
FFN_DENSE = 2816
N_EXPERTS, FFN_EXPERT = 8, 3584

LANES = 128
SUBLANES = 8
ROW_CHUNKS = D_MODEL // LANES
VMEM_LIMIT = 56 * 1024 * 1024

ROW_TILE = 512
GLA_STEP = 512
RET_CHUNK = 512
FFN_CHUNK = 256
DIL_GROUPS = 8
MOE_ROWS = 512
MOE_FCHUNK = 1792
MOE_INNER = 256
GATHER_ROWS = 512
DISPATCH_ROWS = 1024

QK_DIL = DIL_HEADS * DIL_HD
GLA_QK = GLA_HEADS * GLA_DK
GLA_V = GLA_HEADS * GLA_DV
EVEN_W = 2 * GLA_QK + 2 * GLA_V + 3 * QK_DIL + LANES
ODD_W = 2 * RET_HEADS * RET_DK + 2 * RET_HEADS * RET_DV


def _dot(a, b):
    return jnp.dot(a, b, preferred_element_type=F32)


def _dot_nt(a, b):
    return lax.dot_general(a, b, (((1,), (1,)), ((), ())), preferred_element_type=F32)


def _dot_tn(a, b):
    return lax.dot_general(a, b, (((0,), (0,)), ((), ())), preferred_element_type=F32)


def _split(x):
    hi = x.astype(BF16)
    lo = (x - hi.astype(F32)).astype(BF16)
    return hi, lo


def _rms(x, gain, eps=NORM_EPS):
    ms = jnp.mean(x * x, axis=-1, keepdims=True)
    return x * lax.rsqrt(ms + eps) * gain


def _silu(x):
    return x / (1.0 + jnp.exp(-x))


def _log_sigmoid(z):
    return jnp.minimum(z, 0.0) - jnp.log1p(jnp.exp(-jnp.abs(z)))


def _params(sem):
    return pltpu.CompilerParams(dimension_semantics=sem, vmem_limit_bytes=VMEM_LIMIT)


def _resident(shape):
    nd = len(shape)
    return pl.BlockSpec(shape, lambda *_: (0,) * nd, pipeline_mode=pl.Buffered(1))


def _head_inv_rms(acc, grp_ref, grpt_ref):
    ss = _dot((acc * acc).astype(BF16), grp_ref[...])
    inv = lax.rsqrt(ss * (1.0 / DIL_HD) + NORM_EPS)
    ih, il = _split(inv)
    return _dot(ih, grpt_ref[...]) + _dot(il, grpt_ref[...])


DIL_STRIDE = 4
assert [d for _, d in DIL_PATTERNS] == [1, DIL_STRIDE, DIL_STRIDE * DIL_STRIDE]


def _store_by_residue(val, nat_ref, r4_ref, r16_ref, sc_ref, sc4_ref):
    nat_ref[...] = val.astype(BF16)
    nc = QK_DIL // LANES
    n4 = ROW_TILE // DIL_STRIDE
    n16 = n4 // DIL_STRIDE
    for c in range(nc):
        sc_ref[c] = val[:, c * LANES:(c + 1) * LANES]
    for b in range(DIL_STRIDE):
        for c in range(nc):
            y = sc_ref[c, pl.ds(b, n4, stride=DIL_STRIDE), :]
            r4_ref[b, :, c * LANES:(c + 1) * LANES] = y.astype(BF16)
            sc4_ref[b, c] = y
    for b in range(DIL_STRIDE):
        for a in range(DIL_STRIDE):
            for c in range(nc):
                r16_ref[DIL_STRIDE * a + b, :, c * LANES:(c + 1) * LANES] = (
                    sc4_ref[b, c, pl.ds(a, n16, stride=DIL_STRIDE), :].astype(BF16))


def _in_even_kernel(x_ref, g_ref, w_ref, qn_ref, kn_ref, grp_ref, grpt_ref,
                    qk_ref, v_ref, r_ref, lr_ref, q1_ref, k1_ref, v1_ref,
                    q4_ref, k4_ref, v4_ref, q16_ref, k16_ref, v16_ref, sc_ref, sc4_ref):
    xn = _rms(x_ref[...], g_ref[...]).astype(BF16)
    c = 2 * GLA_QK + 2 * GLA_V
    qb = _dot(xn, w_ref[:, c:c + QK_DIL])
    qb = qb * _head_inv_rms(qb, grp_ref, grpt_ref) * qn_ref[...]
    _store_by_residue(qb, q1_ref, q4_ref, q16_ref, sc_ref.at[0], sc4_ref.at[0])
    c += QK_DIL
    kb = _dot(xn, w_ref[:, c:c + QK_DIL])
    kb = kb * _head_inv_rms(kb, grp_ref, grpt_ref) * kn_ref[...]
    _store_by_residue(kb, k1_ref, k4_ref, k16_ref, sc_ref.at[1], sc4_ref.at[1])
    c += QK_DIL
    _store_by_residue(_dot(xn, w_ref[:, c:c + QK_DIL]), v1_ref, v4_ref, v16_ref, sc_ref.at[2], sc4_ref.at[2])
    c += QK_DIL
    lr_ref[...] = _dot(xn, w_ref[:, c:c + LANES])
    c = 0
    for ref, width in ((qk_ref, 2 * GLA_QK), (v_ref, GLA_V)):
        ref[...] = _dot(xn, w_ref[:, c:c + width]).astype(BF16)
        c += width
    r_ref[...] = _silu(_dot(xn, w_ref[:, c:c + GLA_V])).astype(BF16)


def _in_even(x2d, gain, w, qn, kn, grp, grpt):
    t = x2d.shape[0]
    tm = ROW_TILE
    row = lambda width: pl.BlockSpec((tm, width), lambda i: (i, 0))
    res_spec = lambda d: pl.BlockSpec((d, tm // d, QK_DIL), lambda i: (0, i, 0))
    res_shape = lambda d: jax.ShapeDtypeStruct((d, t // d, QK_DIL), BF16)
    dils = [d for _, d in DIL_PATTERNS[1:]]
    out_shape = (
        jax.ShapeDtypeStruct((t, 2 * GLA_QK), BF16),
        jax.ShapeDtypeStruct((t, GLA_V), BF16),
        jax.ShapeDtypeStruct((t, GLA_V), BF16),
        jax.ShapeDtypeStruct((t, LANES), F32),
    ) + (jax.ShapeDtypeStruct((t, QK_DIL), BF16),) * 3 + tuple(res_shape(d) for d in dils for _ in range(3))
    return pl.pallas_call(
        _in_even_kernel,
        grid=(t // tm,),
        in_specs=[row(D_MODEL), _resident((1, D_MODEL)), _resident((D_MODEL, EVEN_W)),
                  _resident((1, QK_DIL)), _resident((1, QK_DIL)),
                  _resident((QK_DIL, LANES)), _resident((LANES, QK_DIL))],
        out_specs=[row(2 * GLA_QK), row(GLA_V), row(GLA_V), row(LANES)] + [row(QK_DIL)] * 3
                  + [res_spec(d) for d in dils for _ in range(3)],
        out_shape=out_shape,
        scratch_shapes=[pltpu.VMEM((3, QK_DIL // LANES, tm, LANES), F32),
                        pltpu.VMEM((3, DIL_STRIDE, QK_DIL // LANES, tm // DIL_STRIDE, LANES), F32)],
        compiler_params=_params(("parallel",)),
        name="in_even",
    )(x2d, gain, w, qn, kn, grp, grpt)


def _gla_kernel(qk_ref, v_ref, r_ref, lr_ref, wgu_ref, bg_ref, on_ref, tri_ref, o_ref, st_ref, oi_ref):
    c = GLA_CHUNK
    chunks = [slice(ci * c, (ci + 1) * c) for ci in range(GLA_STEP // c)]
    heads = [(slice(h * GLA_DK, (h + 1) * GLA_DK), slice(h * GLA_DV, (h + 1) * GLA_DV))
             for h in range(GLA_HEADS)]

    @pl.when(pl.program_id(1) == 0)
    def _():
        st_ref[...] = jnp.zeros_like(st_ref)

    z = _dot(lr_ref[...].astype(BF16), wgu_ref[...]) + bg_ref[...]
    log_a = _log_sigmoid(z) * (1.0 / GLA_TAU)
    hi, lo = _split(log_a)
    cum = _dot(tri_ref[...], hi) + _dot(tri_ref[...], lo)
    q = qk_ref[:, 0:GLA_QK].astype(F32) * GLA_DK ** -0.5
    k = qk_ref[:, GLA_QK:2 * GLA_QK].astype(F32)
    q_dec = (q * jnp.exp(cum)).astype(BF16)
    k_inv = (k * jnp.exp(-cum)).astype(BF16)
    totals = [cum[rs.stop - 1:rs.stop, :] for rs in chunks]
    k_end = [(k[rs] * jnp.exp(tot - cum[rs])).astype(BF16) for rs, tot in zip(chunks, totals)]
    e_tot = [jnp.exp(tot) for tot in totals]

    rows = lax.broadcasted_iota(jnp.int32, (c, c), 0)
    cols = lax.broadcasted_iota(jnp.int32, (c, c), 1)
    causal = rows >= cols
    for rs in chunks:
        for ks, vs in heads:
            att = jnp.where(causal, _dot_nt(q_dec[rs, ks], k_inv[rs, ks]), 0.0).astype(BF16)
            oi_ref[rs, vs] = _dot(att, v_ref[rs, vs])

    for ci, rs in enumerate(chunks):
        for h, (ks, vs) in enumerate(heads):
            st = st_ref[h]
            oi_ref[rs, vs] += _dot_nt(q_dec[rs, ks], st.astype(BF16))
            st_ref[h] = st * e_tot[ci][:, ks] + _dot_tn(v_ref[rs, vs], k_end[ci][:, ks])

    for _, vs in heads:
        o_ref[:, vs] = (_rms(oi_ref[:, vs], on_ref[...]) * r_ref[:, vs].astype(F32)).astype(BF16)


def _gla(qk, v, r, lr, wgu, bg, on, bsz, seq):
    t = qk.shape[0]
    ns = seq // GLA_STEP
    row = lambda width: pl.BlockSpec((GLA_STEP, width), lambda b, s: (b * ns + s, 0))
    pos = np.arange(GLA_STEP)
    same_chunk = pos[:, None] // GLA_CHUNK == pos[None, :] // GLA_CHUNK
    tri = jnp.asarray(same_chunk & (pos[:, None] >= pos[None, :]), BF16)
    return pl.pallas_call(
        _gla_kernel,
        grid=(bsz, ns),
        in_specs=[row(2 * GLA_QK), row(GLA_V), row(GLA_V), row(LANES),
                  _resident((LANES, GLA_QK)), _resident((1, GLA_QK)), _resident((1, GLA_DV)),
                  _resident((GLA_STEP, GLA_STEP))],
        out_specs=row(GLA_V),
        out_shape=jax.ShapeDtypeStruct((t, GLA_V), BF16),
        scratch_shapes=[pltpu.VMEM((GLA_HEADS, GLA_DV, GLA_DK), F32), pltpu.VMEM((GLA_STEP, GLA_V), F32)],
        compiler_params=_params(("parallel", "arbitrary")),
        name="gla",
    )(qk, v, r, lr, wgu, bg, on, tri)


def _dil_kernel(q_ref, k_ref, v_ref, e_ref, b_ref, a_ref, o_ref, lse_ref, kp_ref, vp_ref):
    blk = DIL_BLOCK
    n = pl.program_id(1)

    @pl.when(n == 0)
    def _():
        kp_ref[...] = jnp.zeros_like(kp_ref)
        vp_ref[...] = jnp.zeros_like(vp_ref)

    qi = lax.broadcasted_iota(jnp.int32, (2 * blk, 2 * blk), 0) & (blk - 1)
    kj = lax.broadcasted_iota(jnp.int32, (2 * blk, 2 * blk), 1)
    valid = ((kj < blk) & (kj >= qi) & (n > 0)) | ((kj >= blk) & (kj - blk <= qi))
    first = lax.broadcasted_iota(jnp.int32, (blk, LANES), 1) < DIL_HD
    ones = jnp.ones((2 * blk, LANES), BF16)
    for g in range(q_ref.shape[0]):
        for c in range(QK_DIL // LANES):
            cs = slice(c * LANES, (c + 1) * LANES)
            q = q_ref[g, :, cs]
            zero = jnp.zeros_like(q)
            q2 = jnp.concatenate([jnp.where(first, q, zero), jnp.where(first, zero, q)], axis=0)
            q_aug = jnp.concatenate([q2, e_ref[...]], axis=1)
            k_aug = jnp.concatenate([jnp.concatenate([kp_ref[g, :, cs], k_ref[g, :, cs]], axis=0), b_ref[c]], axis=1)
            v_aug = jnp.concatenate([jnp.concatenate([vp_ref[g, :, cs], v_ref[g, :, cs]], axis=0), ones], axis=1)
            s = jnp.where(valid, _dot_nt(q_aug, k_aug), NEG_INF)
            m = jnp.max(s, axis=-1, keepdims=True)
            p = jnp.exp(s - m).astype(BF16)
            ov = _dot(p, v_aug)
            den = ov[:, LANES:]
            o2 = ov[:, :LANES] / den
            l2 = m + jnp.log(den) + a_ref[c]
            o_ref[g, :, cs] = jnp.where(first, o2[:blk], o2[blk:]).astype(BF16)
            lse_ref[g, :, cs] = jnp.where(first, l2[:blk], l2[blk:])
    kp_ref[...] = k_ref[...]
    vp_ref[...] = v_ref[...]


def _dilated_tables(dilation):
    blk = DIL_BLOCK
    slopes = np.exp2(-8.0 * np.arange(1, DIL_HEADS + 1) / DIL_HEADS)
    npair = QK_DIL // LANES
    e = np.zeros((2 * blk, LANES), np.float32)
    e[:blk, 0] = 1.0
    e[blk:, 1] = 1.0
    b = np.zeros((npair, 2 * blk, LANES), np.float32)
    a = np.zeros((npair, 2 * blk, LANES), np.float32)
    key_off = np.arange(2 * blk, dtype=np.float64) - blk
    qpos = np.arange(blk, dtype=np.float64)
    for c in range(npair):
        for half in range(2):
            sl = slopes[2 * c + half] * dilation
            b[c, :, half] = sl * key_off
            a[c, half * blk:(half + 1) * blk, :] = (-sl * qpos)[:, None]
    return jnp.asarray(e, BF16), jnp.asarray(b, BF16), jnp.asarray(a, F32)


def _dilated_pass(q, k, v, bsz, seq, dilation):
    seg = seq // dilation
    nb = seg // DIL_BLOCK
    ng = dilation * bsz
    groups = min(DIL_GROUPS, ng)
    assert ng % groups == 0 and seg % DIL_BLOCK == 0
    e, b, a = _dilated_tables(dilation)
    blk = pl.BlockSpec((groups, DIL_BLOCK, QK_DIL), lambda gi, n: (gi, n, 0))
    scratch = pltpu.VMEM((groups, DIL_BLOCK, QK_DIL), BF16)
    return pl.pallas_call(
        _dil_kernel,
        grid=(ng // groups, nb),
        in_specs=[blk, blk, blk, _resident(e.shape), _resident(b.shape), _resident(a.shape)],
        out_specs=[blk, blk],
        out_shape=(jax.ShapeDtypeStruct((ng, seg, QK_DIL), BF16), jax.ShapeDtypeStruct((ng, seg, QK_DIL), F32)),
        scratch_shapes=[scratch, scratch],
        compiler_params=_params(("parallel", "arbitrary")),
        name=f"dilated_d{dilation}",
    )(q, k, v, e, b, a)


def _to_natural(r4_ref, r16_ref, sc4_ref, sc16_ref, tmp_ref):
    nc = QK_DIL // LANES
    n4 = ROW_TILE // DIL_STRIDE
    n16 = n4 // DIL_STRIDE
    for b in range(DIL_STRIDE):
        for c in range(nc):
            cs = slice(c * LANES, (c + 1) * LANES)
            sc4_ref[c, pl.ds(b, n4, stride=DIL_STRIDE), :] = r4_ref[b, :, cs].astype(F32)
            for a in range(DIL_STRIDE):
                tmp_ref[b, c, pl.ds(a, n16, stride=DIL_STRIDE), :] = r16_ref[DIL_STRIDE * a + b, :, cs].astype(F32)
    for b in range(DIL_STRIDE):
        for c in range(nc):
            sc16_ref[c, pl.ds(b, n4, stride=DIL_STRIDE), :] = tmp_ref[b, c]


def _out_even_kernel(x_ref, oa_ref, o1_ref, l1_ref, o4_ref, l4_ref, o16_ref, l16_ref,
                     wa_ref, wb_ref, g_ref, wg_ref, wu_ref, wd_ref, y_ref,
                     so4_ref, sl4_ref, so16_ref, sl16_ref, to_ref, tl_ref, ob_ref, xn_ref, h_ref):
    _to_natural(o4_ref, o16_ref, so4_ref, so16_ref, to_ref)
    _to_natural(l4_ref, l16_ref, sl4_ref, sl16_ref, tl_ref)
    for c in range(QK_DIL // LANES):
        cs = slice(c * LANES, (c + 1) * LANES)
        l1, l2, l3 = l1_ref[:, cs], sl4_ref[c], sl16_ref[c]
        m = jnp.maximum(jnp.maximum(l1, l2), l3)
        e1, e2, e3 = jnp.exp(l1 - m), jnp.exp(l2 - m), jnp.exp(l3 - m)
        ob = (e1 * o1_ref[:, cs].astype(F32) + e2 * so4_ref[c] + e3 * so16_ref[c]) / (e1 + e2 + e3)
        ob_ref[:, cs] = ob.astype(BF16)
    y_ref[...] = x_ref[...] + _dot(oa_ref[...], wa_ref[...]) + _dot(ob_ref[...], wb_ref[...])
    xn_ref[...] = _rms(y_ref[...], g_ref[...]).astype(BF16)
    for ci in range(FFN_DENSE // FFN_CHUNK):
        cs = slice(ci * FFN_CHUNK, (ci + 1) * FFN_CHUNK)
        xn = xn_ref[...]
        h_ref[:, cs] = (_silu(_dot(xn, wg_ref[:, cs])) * _dot(xn, wu_ref[:, cs])).astype(BF16)
    y_ref[...] += _dot(h_ref[...], wd_ref[...])


def _out_even_ffn(x2d, oa, passes, wa, wb, gain, wg, wu, wd):
    t = x2d.shape[0]
    tm = ROW_TILE
    row = lambda width: pl.BlockSpec((tm, width), lambda i: (i, 0))
    res = lambda d: pl.BlockSpec((d, tm // d, QK_DIL), lambda i: (0, i, 0))
    (o1, l1), (o4, l4), (o16, l16) = passes
    d4, d16 = DIL_PATTERNS[1][1], DIL_PATTERNS[2][1]
    sc = pltpu.VMEM((QK_DIL // LANES, tm, LANES), F32)
    tmp = pltpu.VMEM((DIL_STRIDE, QK_DIL // LANES, tm // DIL_STRIDE, LANES), F32)
    return pl.pallas_call(
        _out_even_kernel,
        grid=(t // tm,),
        in_specs=[row(D_MODEL), row(GLA_V), row(QK_DIL), row(QK_DIL), res(d4), res(d4), res(d16), res(d16),
                  _resident((GLA_V, D_MODEL)), _resident((QK_DIL, D_MODEL)), _resident((1, D_MODEL)),
                  _resident((D_MODEL, FFN_DENSE)), _resident((D_MODEL, FFN_DENSE)),
                  _resident((FFN_DENSE, D_MODEL))],
        out_specs=row(D_MODEL),
        out_shape=jax.ShapeDtypeStruct((t, D_MODEL), F32),
        scratch_shapes=[sc, sc, sc, sc, tmp, tmp, pltpu.VMEM((tm, QK_DIL), BF16),
                        pltpu.VMEM((tm, D_MODEL), BF16), pltpu.VMEM((tm, FFN_DENSE), BF16)],
        compiler_params=_params(("parallel",)),
        name="out_even_ffn",
    )(x2d, oa, o1.reshape(t, QK_DIL), l1.reshape(t, QK_DIL), o4, l4, o16, l16, wa, wb, gain, wg, wu, wd)


def _in_odd_kernel(x_ref, g_ref, w_ref, h_ref):
    xn = _rms(x_ref[...], g_ref[...]).astype(BF16)
    width = RET_HEADS * RET_DK
    for ci in range(ODD_W // width):
        cs = slice(ci * width, (ci + 1) * width)
        acc = _dot(xn, w_ref[:, cs])
        if ci == 1:
            acc = acc * RET_DK ** -0.5
        if ci * width >= ODD_W - RET_HEADS * RET_DV:
            acc = _silu(acc)
        h_ref[:, cs] = acc.astype(BF16)


def _in_odd(x2d, gain, w):
    t = x2d.shape[0]
    tm = ROW_TILE
    return pl.pallas_call(
        _in_odd_kernel,
        grid=(t // tm,),
        in_specs=[pl.BlockSpec((tm, D_MODEL), lambda i: (i, 0)), _resident((1, D_MODEL)),
                  _resident((D_MODEL, ODD_W))],
        out_specs=pl.BlockSpec((tm, ODD_W), lambda i: (i, 0)),
        out_shape=jax.ShapeDtypeStruct((t, ODD_W), BF16),
        compiler_params=_params(("parallel",)),
        name="in_odd",
    )(x2d, gain, w)


def _ret_kernel(h_ref, dec_ref, qd_ref, kd_ref, cd_ref, gn_ref, o_ref, st_ref):
    @pl.when(pl.program_id(1) == 0)
    def _():
        st_ref[...] = jnp.zeros_like(st_ref)

    nk = RET_HEADS * RET_DK
    for h in range(RET_HEADS):
        ks = slice(h * RET_DK, (h + 1) * RET_DK)
        vs = slice(h * RET_DV, (h + 1) * RET_DV)
        q = h_ref[:, ks]
        k = h_ref[:, nk + h * RET_DK:nk + (h + 1) * RET_DK]
        v = h_ref[:, 2 * nk + h * RET_DV:2 * nk + (h + 1) * RET_DV]
        g = h_ref[:, 2 * nk + (RET_HEADS + h) * RET_DV:2 * nk + (RET_HEADS + h + 1) * RET_DV]
        att = (_dot_nt(q, k) * dec_ref[h]).astype(BF16)
        q_dec = (q.astype(F32) * qd_ref[h]).astype(BF16)
        k_dec = (k.astype(F32) * kd_ref[h]).astype(BF16)
        st = st_ref[h]
        o = _dot(att, v) + _dot(q_dec, st.astype(BF16))
        st_ref[h] = st * cd_ref[h] + _dot_tn(k_dec, v)
        mu = jnp.mean(o, axis=-1, keepdims=True)
        oc = o - mu
        var = jnp.mean(oc * oc, axis=-1, keepdims=True)
        o_ref[:, vs] = (oc * lax.rsqrt(var + GN_EPS) * gn_ref[:, vs] * g.astype(F32)).astype(BF16)


def _retention_tables():
    h, c = RET_HEADS, RET_CHUNK
    log_gamma = np.log1p(-np.exp2(-5.0 - np.arange(h, dtype=np.float64)))
    pos = np.arange(c, dtype=np.float64)
    rel = pos[:, None] - pos[None, :]
    decay = np.where(rel >= 0, np.exp(log_gamma[:, None, None] * np.maximum(rel, 0.0)), 0.0)
    q_dec = np.exp(log_gamma[:, None] * (pos + 1.0))[:, :, None]
    k_dec = np.exp(log_gamma[:, None] * (c - 1.0 - pos))[:, :, None]
    chunk_decay = np.broadcast_to(np.exp(log_gamma * c)[:, None, None], (h, 1, RET_DV))
    f = lambda a: jnp.asarray(np.ascontiguousarray(a), F32)
    return f(decay), f(q_dec), f(k_dec), f(chunk_decay)


def _retention(hcat, gn_w, bsz, seq):
    t = hcat.shape[0]
    c = RET_CHUNK
    ns = seq // c
    decay, q_dec, k_dec, chunk_decay = _retention_tables()
    return pl.pallas_call(
        _ret_kernel,
        grid=(bsz, ns),
        in_specs=[pl.BlockSpec((c, ODD_W), lambda b, s: (b * ns + s, 0)),
                  _resident((RET_HEADS, c, c)), _resident((RET_HEADS, c, 1)),
                  _resident((RET_HEADS, c, 1)), _resident((RET_HEADS, 1, RET_DV)),
                  _resident((1, RET_HEADS * RET_DV))],
        out_specs=pl.BlockSpec((c, RET_HEADS * RET_DV), lambda b, s: (b * ns + s, 0)),
        out_shape=jax.ShapeDtypeStruct((t, RET_HEADS * RET_DV), BF16),
        scratch_shapes=[pltpu.VMEM((RET_HEADS, RET_DK, RET_DV), F32)],
        compiler_params=_params(("parallel", "arbitrary")),
        name="retention",
    )(hcat, decay, q_dec, k_dec, chunk_decay, gn_w)


def _out_odd_kernel(x_ref, oc_ref, w_ref, g_ref, wr_ref, tri_ref,
                    y_ref, xn_ref, meta_ref, gate_ref, cnt_ref, run_ref):
    @pl.when(pl.program_id(0) == 0)
    def _():
        run_ref[...] = jnp.zeros_like(run_ref)

    y = x_ref[...] + _dot(oc_ref[...], w_ref[...])
    y_ref[...] = y
    xn = _rms(y, g_ref[...])
    for s in range(ROW_CHUNKS):
        xn_ref[_chunk_rows(s, ROW_TILE), :] = xn[:, s * LANES:(s + 1) * LANES]

    hi, lo = _split(xn)
    both = _dot(hi, wr_ref[...])
    logits = both[:, :LANES] + both[:, LANES:] + _dot(lo, wr_ref[:, :LANES])
    lane = lax.broadcasted_iota(jnp.int32, logits.shape, 1).astype(F32)
    logits = jnp.where(lane < N_EXPERTS, logits, -jnp.inf)
    t1 = jnp.max(logits, axis=-1, keepdims=True)
    i1 = jnp.min(jnp.where(logits == t1, lane, float(LANES)), axis=-1, keepdims=True)
    rest = jnp.where(lane == i1, -jnp.inf, logits)
    t2 = jnp.max(rest, axis=-1, keepdims=True)
    i2 = jnp.min(jnp.where(rest == t2, lane, float(LANES)), axis=-1, keepdims=True)
    e2 = jnp.exp(t2 - t1)
    den = 1.0 + e2
    gate_ref[...] = jnp.where(lane == 0.0, 1.0 / den, jnp.where(lane == 1.0, e2 / den, 0.0))

    sel = (lane == i1) | (lane == i2)
    rank = _dot(tri_ref[...], sel.astype(BF16)) + run_ref[...]
    r1 = jnp.sum(jnp.where(lane == i1, rank, 0.0), axis=-1, keepdims=True)
    r2 = jnp.sum(jnp.where(lane == i2, rank, 0.0), axis=-1, keepdims=True)
    run_ref[...] += jnp.sum(sel.astype(F32), axis=0, keepdims=True)
    cnt_ref[...] = run_ref[...]
    meta = jnp.where(lane == 0.0, i1, jnp.where(lane == 1.0, i2,
                     jnp.where(lane == 2.0, r1, jnp.where(lane == 3.0, r2, 0.0))))
    meta_ref[...] = meta.T[:SUBLANES, :]


def _out_odd(x2d, oc, w, gain, wr):
    t = x2d.shape[0]
    tm = ROW_TILE
    tri = jnp.asarray(np.tril(np.ones((tm, tm), np.float32), -1), BF16)
    row = lambda width: pl.BlockSpec((tm, width), lambda i: (i, 0))
    return pl.pallas_call(
        _out_odd_kernel,
        grid=(t // tm,),
        in_specs=[row(D_MODEL), row(RET_HEADS * RET_DV), _resident((RET_HEADS * RET_DV, D_MODEL)),
                  _resident((1, D_MODEL)), _resident((D_MODEL, 2 * LANES)), _resident((tm, tm))],
        out_specs=[row(D_MODEL), pl.BlockSpec((tm * ROW_CHUNKS, LANES), lambda i: (i, 0)),
                   pl.BlockSpec((SUBLANES, tm), lambda i: (0, i)), row(LANES),
                   pl.BlockSpec((1, LANES), lambda i: (0, 0))],
        out_shape=(jax.ShapeDtypeStruct((t, D_MODEL), F32),
                   jax.ShapeDtypeStruct((t * ROW_CHUNKS, LANES), F32),
                   jax.ShapeDtypeStruct((SUBLANES, t), F32),
                   jax.ShapeDtypeStruct((t, LANES), F32),
                   jax.ShapeDtypeStruct((1, LANES), F32)),
        scratch_shapes=[pltpu.VMEM((1, LANES), F32)],
        compiler_params=_params(("arbitrary",)),
        name="out_odd_router",
    )(x2d, oc, w, gain, wr, tri)


def _chunk_rows(s, n_rows):
    return pl.ds(s, n_rows, stride=ROW_CHUNKS)


def _tile_row(ref, row):
    return ref.at[pl.ds(pl.multiple_of(row * ROW_CHUNKS, ROW_CHUNKS), ROW_CHUNKS), :]


def _row_copy(src_ref, src_row, dst_ref, dst_row, sem):
    return pltpu.make_async_copy(_tile_row(src_ref, src_row), _tile_row(dst_ref, dst_row), sem)


def _dispatch_kernel(zs_ref, zon_ref, dest_ref, x_ref, xs_ref, zero_ref, sem):
    n = DISPATCH_ROWS

    @pl.when(pl.program_id(0) == 0)
    def _():
        zero_ref[...] = jnp.zeros_like(zero_ref)
        blocks = [pltpu.make_async_copy(
            zero_ref, xs_ref.at[pl.ds(pl.multiple_of(zs_ref[e] * ROW_CHUNKS, ROW_CHUNKS),
                                      MOE_ROWS * ROW_CHUNKS), :], sem) for e in range(2 * N_EXPERTS)]
        for e, cp in enumerate(blocks):
            pl.when(zon_ref[e] > 0)(cp.start)
        for e, cp in enumerate(blocks):
            pl.when(zon_ref[e] > 0)(cp.wait)

    def issue(j, carry):
        _row_copy(x_ref, j, xs_ref, dest_ref[0, 0, j], sem).start(priority=0)
        _row_copy(x_ref, j, xs_ref, dest_ref[0, 0, n + j], sem).start(priority=1)
        return carry

    lax.fori_loop(0, n, issue, 0, unroll=8)

    def drain(j, carry):
        _row_copy(x_ref, 0, xs_ref, 0, sem).wait()
        return carry

    lax.fori_loop(0, 2 * n, drain, 0, unroll=8)


def _dispatch(xn3, dest_blocks, zero_starts, zero_on, n_rows):
    t = xn3.shape[0] // ROW_CHUNKS
    n = DISPATCH_ROWS
    grid_spec = pltpu.PrefetchScalarGridSpec(
        num_scalar_prefetch=2,
        grid=(t // n,),
        in_specs=[pl.BlockSpec((1, 1, 2 * n), lambda i, zs, zon: (i, 0, 0), memory_space=pltpu.SMEM),
                  pl.BlockSpec((n * ROW_CHUNKS, LANES), lambda i, zs, zon: (i, 0))],
        out_specs=pl.BlockSpec(memory_space=pl.ANY),
        scratch_shapes=[pltpu.VMEM((MOE_ROWS * ROW_CHUNKS, LANES), F32), pltpu.SemaphoreType.DMA(())],
    )
    return pl.pallas_call(
        _dispatch_kernel,
        grid_spec=grid_spec,
        out_shape=jax.ShapeDtypeStruct((n_rows * ROW_CHUNKS, LANES), F32),
        compiler_params=_params(("arbitrary",)),
        name="moe_dispatch",
    )(zero_starts, zero_on, dest_blocks, xn3)


def _moe_kernel(be_ref, nv_ref, x_ref, wg_ref, wu_ref, wd_ref, y_ref, xb_ref, h_ref, acc_ref):
    del be_ref
    b = pl.program_id(0)
    f = pl.program_id(1)
    nf = pl.num_programs(1)

    valid = b < nv_ref[0]

    @pl.when(f == 0)
    def _():
        acc_ref[...] = jnp.zeros_like(acc_ref)

    @pl.when(valid & (f == 0))
    def _():
        for s in range(ROW_CHUNKS):
            xb_ref[:, s * LANES:(s + 1) * LANES] = x_ref[_chunk_rows(s, MOE_ROWS), :].astype(BF16)

    @pl.when(valid)
    def _():
        for ci in range(MOE_FCHUNK // MOE_INNER):
            cs = slice(ci * MOE_INNER, (ci + 1) * MOE_INNER)
            x = xb_ref[...]
            h_ref[:, cs] = (_silu(_dot(x, wg_ref[0, :, cs])) * _dot(x, wu_ref[0, :, cs])).astype(BF16)
        acc_ref[...] += _dot(h_ref[...], wd_ref[0])

    @pl.when(f == nf - 1)
    def _():
        for s in range(ROW_CHUNKS):
            y_ref[_chunk_rows(s, MOE_ROWS), :] = acc_ref[:, s * LANES:(s + 1) * LANES]


def _moe_experts(xs, block_expert, n_valid, wg, wu, wd):
    n_rows = xs.shape[0] // ROW_CHUNKS
    bm, tf = MOE_ROWS, MOE_FCHUNK
    nf = FFN_EXPERT // tf
    fidx = lambda b, f, nv: jnp.where(b < nv[0], f, nf - 1)
    rows = pl.BlockSpec((bm * ROW_CHUNKS, LANES), lambda b, f, be, nv: (b, 0))
    grid_spec = pltpu.PrefetchScalarGridSpec(
        num_scalar_prefetch=2,
        grid=(n_rows // bm, nf),
        in_specs=[rows,
                  pl.BlockSpec((1, D_MODEL, tf), lambda b, f, be, nv: (be[b], 0, fidx(b, f, nv))),
                  pl.BlockSpec((1, D_MODEL, tf), lambda b, f, be, nv: (be[b], 0, fidx(b, f, nv))),
                  pl.BlockSpec((1, tf, D_MODEL), lambda b, f, be, nv: (be[b], fidx(b, f, nv), 0))],
        out_specs=rows,
        scratch_shapes=[pltpu.VMEM((bm, D_MODEL), BF16), pltpu.VMEM((bm, tf), BF16),
                        pltpu.VMEM((bm, D_MODEL), F32)],
    )
    return pl.pallas_call(
        _moe_kernel,
        grid_spec=grid_spec,
        out_shape=jax.ShapeDtypeStruct((n_rows * ROW_CHUNKS, LANES), F32),
        compiler_params=_params(("arbitrary", "arbitrary")),
        name="moe_experts",
    )(block_expert, n_valid, xs, wg, wu, wd)


def _combine_kernel(dcur_ref, dnext_ref, x_ref, gate_ref, y_ref, o_ref, buf_ref, sem):
    n = GATHER_ROWS
    i = pl.program_id(0)
    slot = i % 2

    def issue(dref, sl):
        def body(j, carry):
            _row_copy(y_ref, dref[0, 0, j], buf_ref.at[sl], j, sem.at[sl]).start(priority=0)
            _row_copy(y_ref, dref[0, 0, n + j], buf_ref.at[sl], n + j, sem.at[sl]).start(priority=1)
            return carry
        lax.fori_loop(0, n, body, 0, unroll=8)

    @pl.when(i == 0)
    def _():
        issue(dcur_ref, 0)

    @pl.when(i + 1 < pl.num_programs(0))
    def _():
        issue(dnext_ref, 1 - slot)

    def drain(j, carry):
        _row_copy(y_ref, 0, buf_ref.at[slot], 0, sem.at[slot]).wait()
        return carry

    lax.fori_loop(0, 2 * n, drain, 0, unroll=8)

    g1 = gate_ref[:, 0:1]
    g2 = gate_ref[:, 1:2]
    for s in range(ROW_CHUNKS):
        cs = slice(s * LANES, (s + 1) * LANES)
        o_ref[:, cs] = (x_ref[:, cs] + g1 * buf_ref[slot, _chunk_rows(s, n), :]
                        + g2 * buf_ref[slot, _chunk_rows(n * ROW_CHUNKS + s, n), :])


def _combine(x3, gates, ys, dest_blocks):
    t = x3.shape[0]
    n = GATHER_ROWS
    nt = t // n
    idx = lambda f: pl.BlockSpec((1, 1, 2 * n), f, memory_space=pltpu.SMEM)
    return pl.pallas_call(
        _combine_kernel,
        grid=(nt,),
        in_specs=[idx(lambda i: (i, 0, 0)), idx(lambda i: (jnp.minimum(i + 1, nt - 1), 0, 0)),
                  pl.BlockSpec((n, D_MODEL), lambda i: (i, 0)),
                  pl.BlockSpec((n, LANES), lambda i: (i, 0)),
                  pl.BlockSpec(memory_space=pl.ANY)],
        out_specs=pl.BlockSpec((n, D_MODEL), lambda i: (i, 0)),
        out_shape=jax.ShapeDtypeStruct((t, D_MODEL), F32),
        scratch_shapes=[pltpu.VMEM((2, 2 * n * ROW_CHUNKS, LANES), F32),
                        pltpu.SemaphoreType.DMA((2,))],
        compiler_params=_params(("arbitrary",)),
        name="moe_combine",
    )(dest_blocks, dest_blocks, x3, gates, ys)


def _even_weights(w_in, q_norm, k_norm):
    a = 2 * GLA_QK + 2 * GLA_V
    pad = jnp.zeros((D_MODEL, LANES - GLA_RANK), w_in.dtype)
    w = jnp.concatenate([w_in[:, :a], w_in[:, a + GLA_RANK:], w_in[:, a:a + GLA_RANK], pad], axis=1)
    qn = (jnp.tile(q_norm, DIL_HEADS) * DIL_HD ** -0.5).reshape(1, QK_DIL)
    kn = jnp.tile(k_norm, DIL_HEADS).reshape(1, QK_DIL)
    grp = np.zeros((QK_DIL, LANES), np.float32)
    grp[np.arange(QK_DIL), np.arange(QK_DIL) // DIL_HD] = 1.0
    return (w.astype(BF16), qn.astype(F32), kn.astype(F32),
            jnp.asarray(grp, BF16), jnp.asarray(grp.T.copy(), BF16))


def kernel(x, mix_norm_even, w_in_even, w_gla_gate_up, b_gla_gate, gla_out_norm, q_norm, k_norm,
           w_out_even, ffn_norm_even, w_ffn_gate, w_ffn_up, w_ffn_down, mix_norm_odd, w_in_odd,
           ret_group_norm, w_out_odd, ffn_norm_odd, w_router, w_exp_gate, w_exp_up, w_exp_down):
    bsz, seq, d = x.shape
    t = bsz * seq
    x2d = x.reshape(t, d)
    row_vec = lambda v: v.reshape(1, -1).astype(F32)

    w_in, qn, kn, grp, grpt = _even_weights(w_in_even[0], q_norm[0], k_norm[0])
    (qk_a, v_a, r_a, lr, q1, k1, v1, q4, k4, v4, q16, k16, v16) = _in_even(
        x2d, row_vec(mix_norm_even[0]), w_in, qn, kn, grp, grpt)
    wgu = jnp.zeros((LANES, GLA_QK), F32).at[:GLA_RANK].set(w_gla_gate_up[0]).astype(BF16)
    o_a = _gla(qk_a, v_a, r_a, lr, wgu, row_vec(b_gla_gate[0]), row_vec(gla_out_norm[0]), bsz, seq)
    seqs = lambda a, dil: a.reshape(dil * bsz, seq // dil, QK_DIL)
    passes = [[o.reshape(dil, t // dil, QK_DIL) for o in
               _dilated_pass(seqs(q, dil), seqs(k, dil), seqs(v, dil), bsz, seq, dil)]
              for (_, dil), (q, k, v) in zip(DIL_PATTERNS, ((q1, k1, v1), (q4, k4, v4), (q16, k16, v16)))]
    w_out = w_out_even[0].astype(BF16)
    x2d = _out_even_ffn(x2d, o_a, passes, w_out[:GLA_V], w_out[GLA_V:], row_vec(ffn_norm_even[0]),
                        w_ffn_gate[0].astype(BF16), w_ffn_up[0].astype(BF16), w_ffn_down[0].astype(BF16))

    hcat = _in_odd(x2d, row_vec(mix_norm_odd[0]), w_in_odd[0].astype(BF16))
    o_c = _retention(hcat, row_vec(ret_group_norm[0]), bsz, seq)
    wr = jnp.zeros((D_MODEL, LANES), F32).at[:, :N_EXPERTS].set(w_router[0])
    wr_hi = wr.astype(BF16)
    wr_lo = (wr - wr_hi.astype(F32)).astype(BF16)
    x3, xn3, meta, gates, counts = _out_odd(x2d, o_c, w_out_odd[0].astype(BF16), row_vec(ffn_norm_odd[0]),
                                            jnp.concatenate([wr_hi, wr_lo], axis=1))

    bm = MOE_ROWS
    n_blocks = t * 2 // bm + N_EXPERTS
    n_rows = n_blocks * bm
    cnt = counts[0, :N_EXPERTS].astype(jnp.int32)
    padded = (cnt + bm - 1) // bm * bm
    pends = jnp.cumsum(padded)
    pstarts = pends - padded
    meta = meta.astype(jnp.int32)
    dest1 = pstarts[meta[0]] + meta[2]
    dest2 = pstarts[meta[1]] + meta[3]

    def dest_blocks(n):
        return jnp.concatenate([dest1.reshape(t // n, n), dest2.reshape(t // n, n)],
                               axis=1).reshape(t // n, 1, 2 * n).astype(jnp.int32)

    block_rows = jnp.arange(n_blocks, dtype=jnp.int32) * bm
    block_expert = jnp.minimum(jnp.sum(pends[None, :] <= block_rows[:, None], axis=1),
                               N_EXPERTS - 1).astype(jnp.int32)
    n_valid = (pends[-1:] // bm).astype(jnp.int32)

    tail = pends[-1] + jnp.arange(N_EXPERTS, dtype=jnp.int32) * bm
    zero_on = jnp.concatenate([padded > 0, tail + bm <= n_rows]).astype(jnp.int32)
    zero_starts = jnp.concatenate([jnp.maximum(pends - bm, 0), jnp.minimum(tail, n_rows - bm)]).astype(jnp.int32)
    xs = _dispatch(xn3, dest_blocks(DISPATCH_ROWS), zero_starts, zero_on, n_rows)
    ys = _moe_experts(xs, block_expert, n_valid, w_exp_gate[0].astype(BF16),
                      w_exp_up[0].astype(BF16), w_exp_down[0].astype(BF16))
    out = _combine(x3, gates, ys, dest_blocks(GATHER_ROWS))
    return out.reshape(bsz, seq, d)
```
